```python
import math
import jax
import jax.numpy as jnp
from jax import lax
import numpy as np


D_MODEL = 1024
BATCH = 2
SEQ = 8192
DEPTH = 4

D_MIX = D_MODEL
GLA_HEADS = 4
GLA_DK = 32
GLA_DV = 64
GLA_W = GLA_HEADS * GLA_DV
GLA_LOWRANK = 16
GLA_GATE_NORM = 16.0
GLA_CHUNK = 32
S5_GROUPS = 16
S5_CH = 16
S5_W = S5_GROUPS * S5_CH
S5_STATE = 64
S5_DT_MIN = 1e-3
S5_DT_MAX = 1e-1
NSA_HEADS = 8
NSA_KV = 2
NSA_HPG = NSA_HEADS // NSA_KV
NSA_DH = 64
NSA_W = NSA_HEADS * NSA_DH
N_BRANCH = 3
CMP_LEN = 32
CMP_STRIDE = 16
CMP_HIDDEN = 256
SEL_BLOCK = 64
SEL_TOPK = 16
WINDOW = 512
Q_BLOCK = 128
RMS_EPS = 1e-6
NEG_INF = -1e30
FORCE_SCORE = 1e4

IN_SIZES = (GLA_HEADS * GLA_DK, GLA_HEADS * GLA_DK, GLA_W, GLA_LOWRANK, GLA_W,
            S5_W, S5_W,
            NSA_W, N_BRANCH * 2 * NSA_KV * NSA_DH, NSA_HEADS * N_BRANCH, NSA_W)
D_IN = sum(IN_SIZES)
IN_SPLITS = tuple(int(v) for v in np.cumsum(IN_SIZES)[:-1])

kernel_name = 'hybrid_gla_s5_nsa_parallel_heads'


def rmsnorm(x, g):
    xf = x.astype(jnp.float32)
    y = xf * lax.rsqrt(jnp.mean(xf * xf, axis=-1, keepdims=True) + RMS_EPS)
    return (y * g.astype(jnp.float32)).astype(x.dtype)


def masked_softmax(s, mask):
    s = jnp.where(mask, s.astype(jnp.float32), NEG_INF)
    return jax.nn.softmax(s, axis=-1) * mask


def gla_mixer(q, k, v, lr, gate, w2, b2, onorm_g):
    B, T, _ = q.shape
    H, C = GLA_HEADS, GLA_CHUNK
    N = T // C
    f32 = jnp.float32
    glog = jax.nn.log_sigmoid(jnp.matmul(lr, w2).astype(f32) + b2.astype(f32)) / GLA_GATE_NORM

    def heads(a, d):
        return a.astype(f32).reshape(B, N, C, H, d).transpose(0, 3, 1, 2, 4)

    qh = heads(q, GLA_DK) * (GLA_DK ** -0.5)
    kh = heads(k, GLA_DK)
    vh = heads(v, GLA_DV)
    bcum = jnp.cumsum(heads(glog, GLA_DK), axis=3)
    blast = bcum[:, :, :, -1:, :]
    causal = np.tril(np.ones((C, C), dtype=bool))[:, :, None]
    rel = bcum[:, :, :, :, None, :] - bcum[:, :, :, None, :, :]
    decay = jnp.exp(jnp.where(causal, rel, -jnp.inf))
    scores = jnp.einsum('bhnid,bhnjd,bhnijd->bhnij', qh, kh, decay)
    o_intra = jnp.einsum('bhnij,bhnjd->bhnid', scores, vh)
    chunk_kv = jnp.einsum('bhncd,bhnce->bhnde', kh * jnp.exp(blast - bcum), vh)
    chunk_decay = jnp.exp(blast[:, :, :, 0, :])

    def step(S, inp):
        dec, kv = inp
        return dec[..., None] * S + kv, S

    S0 = jnp.zeros((B, H, GLA_DK, GLA_DV), f32)
    _, S_prev = lax.scan(step, S0, (jnp.moveaxis(chunk_decay, 2, 0), jnp.moveaxis(chunk_kv, 2, 0)))
    S_prev = jnp.moveaxis(S_prev, 0, 2)
    o_inter = jnp.einsum('bhncd,bhnde->bhnce', qh * jnp.exp(bcum), S_prev)
    o = (o_intra + o_inter).transpose(0, 2, 3, 1, 4).reshape(B, T, H, GLA_DV)
    o = rmsnorm(o, onorm_g).reshape(B, T, GLA_W)
    return o * jax.nn.silu(gate.astype(f32))


def _complex_affine_combine(e1, e2):
    a1r, a1i, b1r, b1i = e1
    a2r, a2i, b2r, b2i = e2
    return (a1r * a2r - a1i * a2i,
            a1r * a2i + a1i * a2r,
            a2r * b1r - a2i * b1i + b2r,
            a2r * b1i + a2i * b1r + b2i)


def s5_mixer(u, gate, lam_re, lam_im, log_step, b_re, b_im, c_re, c_im, d_skip, glu_w, glu_b):
    B, T, _ = u.shape
    f32 = jnp.float32
    uf = u.astype(f32).reshape(B, T, S5_GROUPS, S5_CH)
    lr = jnp.minimum(lam_re.astype(f32), -1e-4)
    li = lam_im.astype(f32)
    dt = jnp.exp(log_step.astype(f32))[:, None]
    mag = jnp.exp(lr * dt)
    ar, ai = mag * jnp.cos(li * dt), mag * jnp.sin(li * dt)
    den = lr * lr + li * li
    fr = ((ar - 1.0) * lr + ai * li) / den
    fi = (ai * lr - (ar - 1.0) * li) / den
    br, bim = b_re.astype(f32), b_im.astype(f32)
    bbar_r = fr[..., None] * br - fi[..., None] * bim
    bbar_i = fr[..., None] * bim + fi[..., None] * br
    bu_r = jnp.einsum('btgc,gpc->btgp', uf, bbar_r)
    bu_i = jnp.einsum('btgc,gpc->btgp', uf, bbar_i)
    a_r = jnp.broadcast_to(ar, bu_r.shape)
    a_i = jnp.broadcast_to(ai, bu_r.shape)
    _, _, xr, xi = lax.associative_scan(_complex_affine_combine, (a_r, a_i, bu_r, bu_i), axis=1)
    y = (jnp.einsum('btgp,gcp->btgc', xr, c_re.astype(f32))
         - jnp.einsum('btgp,gcp->btgc', xi, c_im.astype(f32))
         + d_skip.astype(f32) * uf)
    y = jax.nn.gelu(y.reshape(B, T, S5_W))
    z = jnp.matmul(y, glu_w.astype(f32)) + glu_b.astype(f32)
    y = z[..., :S5_W] * jax.nn.sigmoid(z[..., S5_W:])
    return y * jax.nn.silu(gate.astype(f32))


def nsa_mixer(q, kv, gate_logit, gate, gate_b, qn_g, kn_g, cmp_pos, cmp_w1, cmp_b1, cmp_w2, cmp_b2):
    B, T, _ = q.shape
    G, Hg, dh = NSA_KV, NSA_HPG, NSA_DH
    f32 = jnp.float32
    qh = rmsnorm(q.reshape(B, T, G, Hg, dh), qn_g) * (dh ** -0.5)
    kv = kv.reshape(B, T, N_BRANCH, 2, G, dh)
    n_cmp = (T - CMP_LEN) // CMP_STRIDE + 1
    blk_idx = np.arange(n_cmp)[:, None] * CMP_STRIDE + np.arange(CMP_LEN)[None, :]
    cmp_start = blk_idx[:, 0]
    cmp_end = blk_idx[:, -1]

    def compress(a, j):
        blocks = a[:, blk_idx] + cmp_pos[j][:, None, :]
        blocks = blocks.transpose(0, 1, 3, 2, 4).reshape(B, n_cmp, G, CMP_LEN * dh)
        hid = jax.nn.gelu(jnp.matmul(blocks, cmp_w1[j]) + cmp_b1[j])
        return jnp.matmul(hid, cmp_w2[j]) + cmp_b2[j]

    k_cmp = rmsnorm(compress(kv[:, :, 0, 0], 0), kn_g[0])
    v_cmp = compress(kv[:, :, 0, 1], 1)
    n_sel = T // SEL_BLOCK
    top_k = min(SEL_TOPK, n_sel)
    sel_start = np.arange(n_sel) * SEL_BLOCK
    overlap = jnp.asarray(((cmp_start[:, None] < sel_start[None, :] + SEL_BLOCK)
                           & (cmp_end[:, None] >= sel_start[None, :])).astype(np.float32))

    def to_blocks(a):
        return a.reshape(B, n_sel, SEL_BLOCK, G, dh).transpose(0, 3, 1, 2, 4)

    k_sel_blk = to_blocks(rmsnorm(kv[:, :, 1, 0], kn_g[1]))
    v_sel_blk = to_blocks(kv[:, :, 1, 1])
    pad = ((0, 0), (WINDOW, 0), (0, 0), (0, 0))
    k_win = jnp.pad(rmsnorm(kv[:, :, 2, 0], kn_g[2]), pad)
    v_win = jnp.pad(kv[:, :, 2, 1], pad)
    gates = jax.nn.sigmoid(gate_logit.astype(f32) + gate_b.astype(f32)).reshape(B, T, G, Hg, N_BRANCH)
    n_qb = T // Q_BLOCK
    q_blocks = qh.reshape(B, n_qb, Q_BLOCK, G, Hg, dh).swapaxes(0, 1)
    g_blocks = gates.reshape(B, n_qb, Q_BLOCK, G, Hg, N_BRANCH).swapaxes(0, 1)
    bi = jnp.arange(B)[:, None, None, None]
    gi = jnp.arange(G)[None, :, None, None]
    sel_ids = np.arange(n_sel)

    def block_fn(args):
        qb, gb, i = args
        t = i * Q_BLOCK + jnp.arange(Q_BLOCK)
        s_c = jnp.einsum('bqghd,bcgd->bgqhc', qb, k_cmp, preferred_element_type=f32)
        p_c = masked_softmax(s_c, (cmp_end[None, :] <= t[:, None])[None, None, :, None, :])
        o_c = jnp.einsum('bgqhc,bcgd->bgqhd', p_c, v_cmp.astype(f32))
        imp = jnp.einsum('bgqhc,cs->bgqs', p_c, overlap)
        cur = (t // SEL_BLOCK)[:, None]
        forced = (sel_ids[None, :] == 0) | (sel_ids[None, :] == cur) | (sel_ids[None, :] == cur - 1)
        imp = jnp.where(forced, FORCE_SCORE, imp)
        imp = jnp.where(sel_ids[None, :] <= cur, imp, NEG_INF)
        top_val, top_idx = lax.top_k(imp, top_k)
        k_g = k_sel_blk[bi, gi, top_idx]
        v_g = v_sel_blk[bi, gi, top_idx]
        tok = top_idx[..., None] * SEL_BLOCK + jnp.arange(SEL_BLOCK)
        m_s = (top_val > 0.5 * NEG_INF)[..., None] & (tok <= t[None, None, :, None, None])
        s_s = jnp.einsum('bqghd,bgqkld->bgqhkl', qb, k_g, preferred_element_type=f32)
        shp = s_s.shape
        p_s = masked_softmax(s_s.reshape(shp[:4] + (-1,)), m_s.reshape(B, G, Q_BLOCK, 1, -1)).reshape(shp)
        o_s = jnp.einsum('bgqhkl,bgqkld->bgqhd', p_s, v_g.astype(f32))
        kw = lax.dynamic_slice_in_dim(k_win, i * Q_BLOCK, Q_BLOCK + WINDOW, axis=1)
        vw = lax.dynamic_slice_in_dim(v_win, i * Q_BLOCK, Q_BLOCK + WINDOW, axis=1)
        kpos = i * Q_BLOCK - WINDOW + jnp.arange(Q_BLOCK + WINDOW)
        m_w = (kpos[None, :] >= 0) & (kpos[None, :] <= t[:, None]) & (kpos[None, :] > t[:, None] - WINDOW)
        s_w = jnp.einsum('bqghd,bkgd->bgqhk', qb, kw, preferred_element_type=f32)
        p_w = masked_softmax(s_w, m_w[None, None, :, None, :])
        o_w = jnp.einsum('bgqhk,bkgd->bgqhd', p_w, vw.astype(f32))
        gb = gb.transpose(0, 2, 1, 3, 4)
        o = gb[..., 0:1] * o_c + gb[..., 1:2] * o_s + gb[..., 2:3] * o_w
        return o.transpose(0, 2, 1, 3, 4)

    o = lax.map(block_fn, (q_blocks, g_blocks, jnp.arange(n_qb)))
    o = o.swapaxes(0, 1).reshape(B, T, NSA_W)
    return o * jax.nn.silu(gate.astype(f32))


def setup_inputs(seed: int = 0) -> dict:
    key = jax.random.key(seed)
    ks = jax.random.split(key, 32)
    L = DEPTH
    f32 = jnp.float32

    def nrm(k, shape, s):
        return jax.random.normal(k, shape, f32) * s

    x = nrm(ks[0], (BATCH, SEQ, D_MODEL), 1.0)
    norm_g = 1.0 + nrm(ks[1], (L, D_MODEL), 0.01)
    w_in = nrm(ks[2], (L, D_MODEL, D_IN), D_MODEL ** -0.5)
    gla_w2 = nrm(ks[3], (L, GLA_LOWRANK, GLA_HEADS * GLA_DK), GLA_LOWRANK ** -0.5)
    gla_b2 = nrm(ks[4], (L, GLA_HEADS * GLA_DK), 0.1)
    gla_onorm = 1.0 + nrm(ks[5], (L, GLA_DV), 0.01)
    n_idx = jnp.arange(S5_STATE, dtype=f32)
    s5_lam_re = -0.5 + nrm(ks[6], (L, S5_GROUPS, S5_STATE), 0.01)
    s5_lam_im = math.pi * n_idx + nrm(ks[7], (L, S5_GROUPS, S5_STATE), 0.01)
    s5_log_step = jax.random.uniform(ks[8], (L, S5_GROUPS), f32, math.log(S5_DT_MIN), math.log(S5_DT_MAX))
    s5_b_re = nrm(ks[9], (L, S5_GROUPS, S5_STATE, S5_CH), (2 * S5_CH) ** -0.5)
    s5_b_im = nrm(ks[10], (L, S5_GROUPS, S5_STATE, S5_CH), (2 * S5_CH) ** -0.5)
    s5_c_re = nrm(ks[11], (L, S5_GROUPS, S5_CH, S5_STATE), (2 * S5_STATE) ** -0.5)
    s5_c_im = nrm(ks[12], (L, S5_GROUPS, S5_CH, S5_STATE), (2 * S5_STATE) ** -0.5)
    s5_d = nrm(ks[13], (L, S5_GROUPS, S5_CH), 1.0)
    s5_glu_w = nrm(ks[14], (L, S5_W, 2 * S5_W), S5_W ** -0.5)
    s5_glu_b = nrm(ks[15], (L, 2 * S5_W), 0.02)
    nsa_gate_b = nrm(ks[16], (L, NSA_HEADS * N_BRANCH), 0.1)
    nsa_qn = 1.0 + nrm(ks[17], (L, NSA_DH), 0.01)
    nsa_kn = 1.0 + nrm(ks[18], (L, N_BRANCH, NSA_DH), 0.01)
    nsa_cmp_pos = nrm(ks[19], (L, 2, CMP_LEN, NSA_DH), 0.1)
    nsa_cmp_w1 = nrm(ks[20], (L, 2, CMP_LEN * NSA_DH, CMP_HIDDEN), (CMP_LEN * NSA_DH) ** -0.5)
    nsa_cmp_b1 = nrm(ks[21], (L, 2, CMP_HIDDEN), 0.02)
    nsa_cmp_w2 = nrm(ks[22], (L, 2, CMP_HIDDEN, NSA_DH), CMP_HIDDEN ** -0.5)
    nsa_cmp_b2 = nrm(ks[23], (L, 2, NSA_DH), 0.02)
    w_out = nrm(ks[24], (L, D_MIX, D_MODEL), D_MIX ** -0.5 * (2 * DEPTH) ** -0.5)
    return {'x': x, 'norm_g': norm_g, 'w_in': w_in,
            'gla_w2': gla_w2, 'gla_b2': gla_b2, 'gla_onorm': gla_onorm,
            's5_lam_re': s5_lam_re, 's5_lam_im': s5_lam_im, 's5_log_step': s5_log_step,
            's5_b_re': s5_b_re, 's5_b_im': s5_b_im, 's5_c_re': s5_c_re, 's5_c_im': s5_c_im,
            's5_d': s5_d, 's5_glu_w': s5_glu_w, 's5_glu_b': s5_glu_b,
            'nsa_gate_b': nsa_gate_b, 'nsa_qn': nsa_qn, 'nsa_kn': nsa_kn, 'nsa_cmp_pos': nsa_cmp_pos,
            'nsa_cmp_w1': nsa_cmp_w1, 'nsa_cmp_b1': nsa_cmp_b1, 'nsa_cmp_w2': nsa_cmp_w2,
            'nsa_cmp_b2': nsa_cmp_b2, 'w_out': w_out}


def reference(x, norm_g, w_in, gla_w2, gla_b2, gla_onorm, s5_lam_re, s5_lam_im, s5_log_step,
              s5_b_re, s5_b_im, s5_c_re, s5_c_im, s5_d, s5_glu_w, s5_glu_b, nsa_gate_b, nsa_qn,
              nsa_kn, nsa_cmp_pos, nsa_cmp_w1, nsa_cmp_b1, nsa_cmp_w2, nsa_cmp_b2, w_out):
    for l in range(DEPTH):
        h = rmsnorm(x, norm_g[l])
        proj = jnp.matmul(h, w_in[l])
        (gq, gk, gv, glr, gg, su, sg, nq, nkv, ngl, ng) = jnp.split(proj, IN_SPLITS, axis=-1)
        y_gla = gla_mixer(gq, gk, gv, glr, gg, gla_w2[l], gla_b2[l], gla_onorm[l])
        y_s5 = s5_mixer(su, sg, s5_lam_re[l], s5_lam_im[l], s5_log_step[l], s5_b_re[l], s5_b_im[l],
                        s5_c_re[l], s5_c_im[l], s5_d[l], s5_glu_w[l], s5_glu_b[l])
        y_nsa = nsa_mixer(nq, nkv, ngl, ng, nsa_gate_b[l], nsa_qn[l], nsa_kn[l], nsa_cmp_pos[l],
                          nsa_cmp_w1[l], nsa_cmp_b1[l], nsa_cmp_w2[l], nsa_cmp_b2[l])
        mix = jnp.concatenate([y_gla, y_s5, y_nsa], axis=-1).astype(x.dtype)
        x = x + jnp.matmul(mix, w_out[l])
    return x
```

```python
import functools
import math

import numpy as np
import jax
import jax.numpy as jnp
from jax import lax
from jax.experimental import pallas as pl
from jax.experimental.pallas import tpu as pltpu

F32 = jnp.float32
BF16 = jnp.bfloat16

D_MODEL = 1024
GLA_HEADS, GLA_DK, GLA_DV = 4, 32, 64
GLA_QK = GLA_HEADS * GLA_DK
GLA_W = GLA_HEADS * GLA_DV
GLA_LOWRANK = 16
GLA_GATE_NORM = 16.0
GLA_CHUNK = 32
S5_GROUPS, S5_CH, S5_STATE = 16, 16, 64
S5_W = S5_GROUPS * S5_CH
S5_L = 16
NSA_HEADS, NSA_KV, NSA_DH = 8, 2, 64
NSA_HPG = NSA_HEADS // NSA_KV
NSA_W = NSA_HEADS * NSA_DH
N_BRANCH = 3
CMP_LEN, CMP_STRIDE, CMP_HIDDEN = 32, 16, 256
SEL_BLOCK, SEL_TOPK = 64, 16
WINDOW = 512
RMS_EPS = 1e-6
NEG_INF = -1e30
FORCE_SCORE = 1e4

LANE = 128
VMEM_LIMIT = 56 * 1024 * 1024

GLA_COLS = GLA_QK * 2 + GLA_W * 2 + LANE
S5_COLS = 2 * S5_W
NSA_KV_W = N_BRANCH * 2 * NSA_KV * NSA_DH
NSA_COLS = NSA_W + NSA_KV_W + NSA_W + LANE


def _cparams(sem):
    return pltpu.CompilerParams(dimension_semantics=sem, vmem_limit_bytes=VMEM_LIMIT)


def _split_bf16(x):
    hi = x.astype(BF16)
    lo = (x - hi.astype(F32)).astype(BF16)
    return hi, lo


def _dot(a, b):
    return jnp.dot(a, b, preferred_element_type=F32)


def _dot_nt(a, b):
    return lax.dot_general(a, b, (((1,), (1,)), ((), ())), preferred_element_type=F32)


def _dot_tn(a, b):
    return lax.dot_general(a, b, (((0,), (0,)), ((), ())), preferred_element_type=F32)


def _sigmoid(x):
    return 1.0 / (1.0 + jnp.exp(-x))


def _silu(x):
    return x * _sigmoid(x)


def _gelu_tanh(x):
    c = math.sqrt(2.0 / math.pi)
    return 0.5 * x * (1.0 + jnp.tanh(c * (x + 0.044715 * (x * x * x))))


def _seg_mean_sq(x, avg):
    hi, lo = _split_bf16(x * x)
    return _dot(hi, avg) + _dot(lo, avg)


def _avg_matrix(width, seg):
    idx = np.arange(width) // seg
    return jnp.asarray((idx[:, None] == idx[None, :]).astype(np.float32) / seg, dtype=BF16)


def _inproj_kernel(x_ref, g_ref, w_ref, ogla_ref, os5_ref, onsa_ref):
    x = x_ref[...]
    ms = jnp.mean(x * x, axis=-1, keepdims=True)
    h = (x * lax.rsqrt(ms + RMS_EPS) * g_ref[...]).astype(BF16)
    ogla_ref[...] = _dot(h, w_ref[:, 0:GLA_COLS])
    os5_ref[...] = _dot(h, w_ref[:, GLA_COLS:GLA_COLS + S5_COLS])
    onsa_ref[...] = _dot(h, w_ref[:, GLA_COLS + S5_COLS:])


def _inproj(x2, norm_g, w_all, tm):
    m = x2.shape[0]
    ncols = w_all.shape[1]
    return pl.pallas_call(
        _inproj_kernel,
        grid=(m // tm,),
        in_specs=[pl.BlockSpec((tm, D_MODEL), lambda i: (i, 0)),
                  pl.BlockSpec((1, D_MODEL), lambda i: (0, 0)),
                  pl.BlockSpec((D_MODEL, ncols), lambda i: (0, 0))],
        out_specs=[pl.BlockSpec((tm, GLA_COLS), lambda i: (i, 0)),
                   pl.BlockSpec((tm, S5_COLS), lambda i: (i, 0)),
                   pl.BlockSpec((tm, NSA_COLS), lambda i: (i, 0))],
        out_shape=[jax.ShapeDtypeStruct((m, GLA_COLS), F32),
                   jax.ShapeDtypeStruct((m, S5_COLS), F32),
                   jax.ShapeDtypeStruct((m, NSA_COLS), F32)],
        compiler_params=_cparams(("parallel",)),
        name="inproj",
    )(x2, norm_g.reshape(1, D_MODEL), w_all)


def _inproj_weight(w_in_l):
    sizes = (GLA_QK, GLA_QK, GLA_W, GLA_LOWRANK, GLA_W, S5_W, S5_W,
             NSA_W, NSA_KV_W, NSA_HEADS * N_BRANCH, NSA_W)
    offs = np.concatenate([[0], np.cumsum(sizes)])
    seg = [w_in_l[:, offs[i]:offs[i + 1]] for i in range(len(sizes))]
    gq, gk, gv, glr, gg, su, sg, nq, nkv, ngl, ng = seg

    def pad(a):
        return jnp.pad(a, ((0, 0), (0, LANE - a.shape[1])))

    return jnp.concatenate([gq, gk, gv, gg, pad(glr), su, sg, nq, nkv, ng, pad(ngl)], axis=1).astype(BF16)


GLA_LEVELS = (16, 8, 4, 2, 1)


def _gla_exponent_matrices(tt):
    r = np.arange(tt)[:, None]
    c = np.arange(tt)[None, :]
    same = (r // GLA_CHUNK) == (c // GLA_CHUNK)
    mats = [same & (c <= r), same & (c > r)]
    for s in GLA_LEVELS:
        blk_r = r // s
        odd = (blk_r % 2) == 1
        mats.append(odd & (c >= blk_r * s) & (c <= r))
        mats.append((~odd) & (c > r) & (c <= blk_r * s + s - 1))
    return jnp.asarray(np.stack(mats).astype(np.float32), dtype=BF16)


def _gla_kernel(x_ref, dm_ref, w2_ref, b2_ref, on_ref, avg_ref, y_ref, st_ref, *, tt):
    nchunk = tt // GLA_CHUNK

    @pl.when(pl.program_id(1) == 0)
    def _():
        st_ref[...] = jnp.zeros_like(st_ref)

    x = x_ref[...]
    q = x[:, 0:GLA_QK] * (GLA_DK ** -0.5)
    k = x[:, GLA_QK:2 * GLA_QK]
    v = x[:, 2 * GLA_QK:2 * GLA_QK + GLA_W]
    gate = x[:, 2 * GLA_QK + GLA_W:2 * GLA_QK + 2 * GLA_W]
    lr = x[:, 2 * GLA_QK + 2 * GLA_W:]
    z = _dot(lr.astype(BF16), w2_ref[...]) + b2_ref[...]
    glog = -(jnp.maximum(-z, 0.0) + jnp.log1p(jnp.exp(-jnp.abs(z)))) / GLA_GATE_NORM

    ghi, glo = _split_bf16(glog)
    gcat = jnp.concatenate([ghi, glo], axis=1)

    def expo(i):
        e = _dot(dm_ref[i], gcat)
        return e[:, :GLA_QK] + e[:, GLA_QK:]

    row = lax.broadcasted_iota(jnp.int32, (tt, GLA_QK), 0)
    lane = lax.broadcasted_iota(jnp.int32, (tt, GLA_QK), 1)
    rr = lax.broadcasted_iota(jnp.int32, (GLA_HEADS * tt, tt), 0) % tt
    cc = lax.broadcasted_iota(jnp.int32, (GLA_HEADS * tt, tt), 1)
    head_masks = [(lane // GLA_DK) == h for h in range(GLA_HEADS)]

    def stack_heads(a):
        return jnp.concatenate([jnp.where(m, a, 0.0) for m in head_masks], axis=0).astype(BF16)

    att = jnp.where(rr == cc, _dot_nt(stack_heads(q), k.astype(BF16)), 0.0)
    for li, s in enumerate(GLA_LEVELS):
        odd = ((row // s) % 2) == 1
        qs = jnp.where(odd, q * jnp.exp(expo(2 + 2 * li)), 0.0)
        ks = jnp.where(odd, 0.0, k * jnp.exp(expo(3 + 2 * li)))
        blk = (rr // (2 * s)) == (cc // (2 * s))
        att = att + jnp.where(blk, _dot_nt(stack_heads(qs), ks.astype(BF16)), 0.0)
    att = att.astype(BF16)

    lane_v = lax.broadcasted_iota(jnp.int32, (tt, GLA_W), 1)
    o = jnp.zeros((tt, GLA_W), F32)
    for h in range(GLA_HEADS):
        vh = jnp.where((lane_v // GLA_DV) == h, v, 0.0).astype(BF16)
        o = o + _dot(att[h * tt:(h + 1) * tt], vh)

    bcum = expo(0)
    brev = expo(1)
    qe = (q * jnp.exp(bcum)).astype(BF16)
    ke = (k * jnp.exp(brev)).astype(BF16)
    vb = v.astype(BF16)
    sr = lax.broadcasted_iota(jnp.int32, (GLA_W, GLA_QK), 0)
    sc = lax.broadcasted_iota(jnp.int32, (GLA_W, GLA_QK), 1)
    st_mask = (sr // GLA_DV) == (sc // GLA_DK)
    st = st_ref[...]
    o_inter = []
    for c in range(nchunk):
        lo, hi = c * GLA_CHUNK, (c + 1) * GLA_CHUNK
        o_inter.append(_dot_nt(qe[lo:hi], st.astype(BF16)))
        kv_t = jnp.where(st_mask, _dot_tn(vb[lo:hi], ke[lo:hi]), 0.0)
        st = st * jnp.exp(bcum[hi - 1:hi, :]) + kv_t
    st_ref[...] = st
    o = o + jnp.concatenate(o_inter, axis=0)

    ms = _seg_mean_sq(o, avg_ref[...])
    y_ref[...] = o * lax.rsqrt(ms + RMS_EPS) * on_ref[...] * _silu(gate)


def _gla(ogla, bsz, w2, b2, onorm, tt):
    m = ogla.shape[0]
    nt = m // bsz // tt
    w2p = jnp.pad(w2, ((0, LANE - GLA_LOWRANK), (0, 0))).astype(BF16)
    dmats = _gla_exponent_matrices(tt)
    return pl.pallas_call(
        functools.partial(_gla_kernel, tt=tt),
        grid=(bsz, nt),
        in_specs=[pl.BlockSpec((tt, GLA_COLS), lambda b, t: (b * nt + t, 0)),
                  pl.BlockSpec(dmats.shape, lambda b, t: (0, 0, 0)),
                  pl.BlockSpec((LANE, GLA_QK), lambda b, t: (0, 0)),
                  pl.BlockSpec((1, GLA_QK), lambda b, t: (0, 0)),
                  pl.BlockSpec((1, GLA_W), lambda b, t: (0, 0)),
                  pl.BlockSpec((GLA_W, GLA_W), lambda b, t: (0, 0))],
        out_specs=pl.BlockSpec((tt, GLA_W), lambda b, t: (b * nt + t, 0)),
        out_shape=jax.ShapeDtypeStruct((m, GLA_W), F32),
        scratch_shapes=[pltpu.VMEM((GLA_W, GLA_QK), F32)],
        compiler_params=_cparams(("arbitrary", "arbitrary")),
        name="gla",
    )(ogla, dmats, w2p, b2.reshape(1, GLA_QK), jnp.tile(onorm, GLA_HEADS).reshape(1, GLA_W),
      _avg_matrix(GLA_W, GLA_DV))


def _s5_operators(lam_re, lam_im, log_step, b_re, b_im, c_re, c_im, n_super):
    lr = jnp.minimum(lam_re.astype(F32), -1e-4)
    li = lam_im.astype(F32)
    dt = jnp.exp(log_step.astype(F32))[:, None]

    def apow(tau):
        tau = jnp.asarray(tau, F32)[..., None, None]
        mag = jnp.exp(lr * dt * tau)
        return mag * jnp.cos(li * dt * tau), mag * jnp.sin(li * dt * tau)

    ar, ai = apow(1.0)
    den = lr * lr + li * li
    fr = ((ar - 1.0) * lr + ai * li) / den
    fi = (ai * lr - (ar - 1.0) * li) / den
    br, bi = b_re.astype(F32), b_im.astype(F32)
    bbr = fr[..., None] * br - fi[..., None] * bi
    bbi = fr[..., None] * bi + fi[..., None] * br
    cr, ci = c_re.astype(F32), c_im.astype(F32)
    L = S5_L
    pr, pi = apow(np.arange(L + 1))
    hp = lax.Precision.HIGHEST
    cpr = cr[None] * jnp.swapaxes(pr, 1, 1)[:, :, None, :] - ci[None] * pi[:, :, None, :]
    cpi = cr[None] * pi[:, :, None, :] + ci[None] * pr[:, :, None, :]
    taps = (jnp.einsum('tgcp,gpd->tgcd', cpr[:L], bbr, precision=hp)
            - jnp.einsum('tgcp,gpd->tgcd', cpi[:L], bbi, precision=hp))
    s_idx = np.arange(L)
    lag = s_idx[None, :] - s_idx[:, None]
    kin = jnp.where((lag >= 0)[:, :, None, None, None], taps[np.clip(lag, 0, L - 1)], 0.0)
    kin = jnp.transpose(kin, (2, 0, 4, 1, 3)).reshape(S5_GROUPS, L * S5_CH, L * S5_CH)
    rev = pr[L - 1 - s_idx], pi[L - 1 - s_idx]
    vre = rev[0][..., None] * bbr[None] - rev[1][..., None] * bbi[None]
    vim = rev[0][..., None] * bbi[None] + rev[1][..., None] * bbr[None]
    vre = jnp.transpose(vre, (1, 0, 3, 2)).reshape(S5_GROUPS, L * S5_CH, S5_STATE)
    vim = jnp.transpose(vim, (1, 0, 3, 2)).reshape(S5_GROUPS, L * S5_CH, S5_STATE)
    wre = jnp.transpose(cpr[1:], (1, 3, 0, 2)).reshape(S5_GROUPS, S5_STATE, L * S5_CH)
    wim = -jnp.transpose(cpi[1:], (1, 3, 0, 2)).reshape(S5_GROUPS, S5_STATE, L * S5_CH)

    def pair_diag(a):
        g, kk, nn = a.shape
        a = a.reshape(g // 2, 2, kk, nn)
        zero = jnp.zeros_like(a[:, 0])
        top = jnp.concatenate([a[:, 0], zero], axis=2)
        bot = jnp.concatenate([zero, a[:, 1]], axis=2)
        return jnp.concatenate([top, bot], axis=1)

    nsteps = max(1, int(math.ceil(math.log2(n_super))))
    spr, spi = apow(float(L) * (2.0 ** np.arange(nsteps)))
    spr = spr.reshape(nsteps, 1, S5_GROUPS * S5_STATE)
    spi = spi.reshape(nsteps, 1, S5_GROUPS * S5_STATE)
    return (kin.astype(BF16), pair_diag(vre).astype(BF16), pair_diag(vim).astype(BF16),
            pair_diag(wre).astype(BF16), pair_diag(wim).astype(BF16), spr, spi)


def _s5_kernel(u_ref, kin_ref, vre_ref, vim_ref, wre_ref, wim_ref, spr_ref, spi_ref, y_ref, xr_ref, xi_ref,
               *, n_super, nsteps):
    npair = S5_GROUPS // 2
    for m in range(npair):
        up = jnp.concatenate([u_ref[0, 2 * m], u_ref[0, 2 * m + 1]], axis=1)
        xr_ref[:, m * LANE:(m + 1) * LANE] = _dot(up, vre_ref[m])
        xi_ref[:, m * LANE:(m + 1) * LANE] = _dot(up, vim_ref[m])
    xr = xr_ref[...]
    xi = xi_ref[...]
    row = lax.broadcasted_iota(jnp.int32, xr.shape, 0)
    for kstep in range(nsteps):
        d = 1 << kstep
        sr = jnp.where(row >= d, pltpu.roll(xr, d, 0), 0.0)
        si = jnp.where(row >= d, pltpu.roll(xi, d, 0), 0.0)
        pr = spr_ref[kstep]
        pi = spi_ref[kstep]
        xr, xi = xr + pr * sr - pi * si, xi + pr * si + pi * sr
    er = jnp.where(row >= 1, pltpu.roll(xr, 1, 0), 0.0).astype(BF16)
    ei = jnp.where(row >= 1, pltpu.roll(xi, 1, 0), 0.0).astype(BF16)
    width = S5_L * S5_CH
    for m in range(npair):
        inter = (_dot(er[:, m * LANE:(m + 1) * LANE], wre_ref[m])
                 + _dot(ei[:, m * LANE:(m + 1) * LANE], wim_ref[m]))
        for j in range(2):
            g = 2 * m + j
            y_ref[0, g] = _dot(u_ref[0, g], kin_ref[g]) + inter[:, j * width:(j + 1) * width]


def _s5_conv(u_t, ops):
    bsz, _, n_super, width = u_t.shape
    kin, vre, vim, wre, wim, spr, spi = ops
    nsteps = spr.shape[0]
    nst = S5_GROUPS * S5_STATE

    def full(a):
        nd = a.ndim
        return pl.BlockSpec(a.shape, lambda b: (0,) * nd)

    return pl.pallas_call(
        functools.partial(_s5_kernel, n_super=n_super, nsteps=nsteps),
        grid=(bsz,),
        in_specs=[pl.BlockSpec((1, S5_GROUPS, n_super, width), lambda b: (b, 0, 0, 0)),
                  full(kin), full(vre), full(vim), full(wre), full(wim), full(spr), full(spi)],
        out_specs=pl.BlockSpec((1, S5_GROUPS, n_super, width), lambda b: (b, 0, 0, 0)),
        out_shape=jax.ShapeDtypeStruct((bsz, S5_GROUPS, n_super, width), F32),
        scratch_shapes=[pltpu.VMEM((n_super, nst), F32), pltpu.VMEM((n_super, nst), F32)],
        compiler_params=_cparams(("arbitrary",)),
        name="s5_conv",
    )(u_t, kin, vre, vim, wre, wim, spr, spi)


def _nsa_prep_kernel(x_ref, qg_ref, kg_ref, gb_ref, avgq_ref, avgk_ref,
                     q_ref, kc_ref, vc_ref, ks_ref, vs_ref, kw_ref, vw_ref, gt_ref, sg_ref):
    x = x_ref[...]
    q = x[:, 0:NSA_W]
    qn = q * lax.rsqrt(_seg_mean_sq(q, avgq_ref[...]) + RMS_EPS) * qg_ref[...] * (NSA_DH ** -0.5)
    for h in range(NSA_HEADS):
        q_ref[h] = qn[:, h * NSA_DH:(h + 1) * NSA_DH].astype(BF16)
    kv = x[:, NSA_W:NSA_W + NSA_KV_W]
    gw = NSA_KV * NSA_DH
    kc_ref[...] = kv[:, 0:gw]
    vc_ref[...] = kv[:, gw:2 * gw]

    def knorm(a, br):
        return a * lax.rsqrt(_seg_mean_sq(a, avgk_ref[...]) + RMS_EPS) * kg_ref[br]

    ksel = knorm(kv[:, 2 * gw:3 * gw], 1)
    vsel = kv[:, 3 * gw:4 * gw]
    kwin = knorm(kv[:, 4 * gw:5 * gw], 2)
    vwin = kv[:, 5 * gw:6 * gw]
    for g in range(NSA_KV):
        sl = slice(g * NSA_DH, (g + 1) * NSA_DH)
        ks_ref[g] = ksel[:, sl].astype(BF16)
        vs_ref[g] = vsel[:, sl].astype(BF16)
        kw_ref[g] = kwin[:, sl].astype(BF16)
        vw_ref[g] = vwin[:, sl].astype(BF16)
    gate = x[:, NSA_W + NSA_KV_W:2 * NSA_W + NSA_KV_W]
    sgate = _silu(gate)
    for h in range(NSA_HEADS):
        sg_ref[h] = sgate[:, h * NSA_DH:(h + 1) * NSA_DH]
    gl = x[:, 2 * NSA_W + NSA_KV_W:]
    sig = _sigmoid(gl + gb_ref[...])
    per_group = NSA_HPG * N_BRANCH
    gt_ref[0] = sig
    gt_ref[1] = pltpu.roll(sig, LANE - per_group, 1)


def _nsa_prep(onsa, qn_g, kn_g, gate_b, tm):
    m = onsa.shape[0]
    gw = NSA_KV * NSA_DH
    head = lambda n: pl.BlockSpec((n, tm, NSA_DH), lambda i: (0, i, 0))
    tok = lambda w: pl.BlockSpec((tm, w), lambda i: (i, 0))
    gbp = jnp.pad(gate_b, (0, LANE - gate_b.shape[0])).reshape(1, LANE)
    return pl.pallas_call(
        _nsa_prep_kernel,
        grid=(m // tm,),
        in_specs=[pl.BlockSpec((tm, NSA_COLS), lambda i: (i, 0)),
                  pl.BlockSpec((1, NSA_W), lambda i: (0, 0)),
                  pl.BlockSpec((N_BRANCH, 1, gw), lambda i: (0, 0, 0)),
                  pl.BlockSpec((1, LANE), lambda i: (0, 0)),
                  pl.BlockSpec((NSA_W, NSA_W), lambda i: (0, 0)),
                  pl.BlockSpec((gw, gw), lambda i: (0, 0))],
        out_specs=[head(NSA_HEADS), tok(gw), tok(gw), head(NSA_KV), head(NSA_KV), head(NSA_KV), head(NSA_KV),
                   pl.BlockSpec((NSA_KV, tm, LANE), lambda i: (0, i, 0)), head(NSA_HEADS)],
        out_shape=[jax.ShapeDtypeStruct((NSA_HEADS, m, NSA_DH), BF16),
                   jax.ShapeDtypeStruct((m, gw), F32), jax.ShapeDtypeStruct((m, gw), F32),
                   jax.ShapeDtypeStruct((NSA_KV, m, NSA_DH), BF16), jax.ShapeDtypeStruct((NSA_KV, m, NSA_DH), BF16),
                   jax.ShapeDtypeStruct((NSA_KV, m, NSA_DH), BF16), jax.ShapeDtypeStruct((NSA_KV, m, NSA_DH), BF16),
                   jax.ShapeDtypeStruct((NSA_KV, m, LANE), F32),
                   jax.ShapeDtypeStruct((NSA_HEADS, m, NSA_DH), F32)],
        compiler_params=_cparams(("parallel",)),
        name="nsa_prep",
    )(onsa, jnp.tile(qn_g, NSA_HEADS).reshape(1, NSA_W),
      jnp.tile(kn_g, (1, NSA_KV)).reshape(N_BRANCH, 1, gw), gbp,
      _avg_matrix(NSA_W, NSA_DH), _avg_matrix(gw, NSA_DH))


def _compress_kernel(a_ref, pt_ref, pb_ref, w1t_ref, w1b_ref, b1_ref, w2_ref, b2_ref, ng_ref, avg_ref, o_ref,
                     *, normalise):
    a = a_ref[0]
    n = a.shape[0]
    h1 = _dot((a + pt_ref[...]).astype(BF16), w1t_ref[...])
    h2 = _dot((a + pb_ref[...]).astype(BF16), w1b_ref[...])
    hid = _gelu_tanh(h1 + pltpu.roll(h2, n - 1, 0) + b1_ref[...])
    out = _dot(hid.astype(BF16), w2_ref[...]) + b2_ref[...]
    if normalise:
        out = out * lax.rsqrt(_seg_mean_sq(out, avg_ref[...]) + RMS_EPS) * ng_ref[...]
    for g in range(NSA_KV):
        o_ref[0, g] = out[:, g * NSA_DH:(g + 1) * NSA_DH].astype(BF16)


def _compress(raw, bsz, pos, w1, b1, w2, b2, norm_g):
    gw = NSA_KV * NSA_DH
    n = raw.shape[0] // bsz // CMP_STRIDE
    a2 = raw.reshape(bsz, n, CMP_STRIDE * gw)
    half = CMP_STRIDE * NSA_DH

    def pos_rows(p):
        return jnp.tile(p[:, None, :], (1, NSA_KV, 1)).reshape(1, CMP_STRIDE * gw)

    def w1_rows(w):
        w = w.reshape(CMP_STRIDE, NSA_DH, CMP_HIDDEN)
        z = jnp.zeros_like(w)
        g0 = jnp.concatenate([w, z], axis=2)
        g1 = jnp.concatenate([z, w], axis=2)
        return jnp.stack([g0, g1], axis=1).reshape(CMP_STRIDE * gw, NSA_KV * CMP_HIDDEN).astype(BF16)

    zero = jnp.zeros_like(w2)
    w2b = jnp.concatenate([jnp.concatenate([w2, zero], axis=1), jnp.concatenate([zero, w2], axis=1)], axis=0)
    normalise = norm_g is not None
    ng = jnp.tile(norm_g if normalise else jnp.ones((NSA_DH,), F32), NSA_KV).reshape(1, gw)
    full = lambda a: pl.BlockSpec(a.shape, lambda b: (0,) * a.ndim)
    args = (pos_rows(pos[:CMP_STRIDE]), pos_rows(pos[CMP_STRIDE:]), w1_rows(w1[:half]), w1_rows(w1[half:]),
            jnp.tile(b1, NSA_KV).reshape(1, -1), w2b.astype(BF16), jnp.tile(b2, NSA_KV).reshape(1, gw), ng,
            _avg_matrix(gw, NSA_DH))
    return pl.pallas_call(
        functools.partial(_compress_kernel, normalise=normalise),
        grid=(bsz,),
        in_specs=[pl.BlockSpec((1, n, CMP_STRIDE * gw), lambda b: (b, 0, 0))] + [full(a) for a in args],
        out_specs=pl.BlockSpec((1, NSA_KV, n, NSA_DH), lambda b: (b, 0, 0, 0)),
        out_shape=jax.ShapeDtypeStruct((bsz, NSA_KV, n, NSA_DH), BF16),
        compiler_params=_cparams(("parallel",)),
        name="nsa_compress",
    )(a2, *args)


def _cmp_attn_kernel(q_ref, kc_ref, vc_ref, ovt_ref, gt_ref, o_ref, sel_ref, *, tq, n_sel, top_k):
    i = pl.program_id(2)
    q4 = q_ref[...].reshape(NSA_HPG * tq, NSA_DH)
    kc = kc_ref[0, 0]
    vc = vc_ref[0, 0]
    n = kc.shape[0]
    s = _dot_nt(q4, kc)
    t_row = i * tq + (lax.broadcasted_iota(jnp.int32, s.shape, 0) % tq)
    c_end = lax.broadcasted_iota(jnp.int32, s.shape, 1) * CMP_STRIDE + (CMP_LEN - 1)
    mask = c_end <= t_row
    sm = jnp.where(mask, s, NEG_INF)
    mx = jnp.max(sm, axis=1, keepdims=True)
    e = jnp.exp(sm - mx)
    real = lax.broadcasted_iota(jnp.int32, s.shape, 1) < (n - 1)
    denom = jnp.sum(jnp.where(real, e, 0.0), axis=1, keepdims=True)
    p = jnp.where(mask, e, 0.0) / denom
    o = _dot(p.astype(BF16), vc)
    gates = gt_ref[0]
    for h in range(NSA_HPG):
        o_ref[h] = o[h * tq:(h + 1) * tq] * gates[:, h * N_BRANCH:h * N_BRANCH + 1]
    psum = p[0:tq]
    for h in range(1, NSA_HPG):
        psum = psum + p[h * tq:(h + 1) * tq]
    imp = _dot_nt(ovt_ref[...], psum.astype(BF16))
    sid = lax.broadcasted_iota(jnp.int32, imp.shape, 0)
    cur = (i * tq + lax.broadcasted_iota(jnp.int32, imp.shape, 1)) // SEL_BLOCK
    forced = (sid == 0) | (sid == cur) | (sid == cur - 1)
    val = jnp.where(forced, FORCE_SCORE, imp)
    val = jnp.where(sid <= cur, val, NEG_INF)
    chosen = jnp.zeros(imp.shape, F32)
    sidf = sid.astype(F32)
    for _ in range(top_k):
        mval = jnp.max(val, axis=0, keepdims=True)
        first = jnp.min(jnp.where(val == mval, sidf, float(n_sel)), axis=0, keepdims=True)
        pick = sidf == first
        chosen = jnp.where(pick & (mval > 0.5 * NEG_INF), 1.0, chosen)
        val = jnp.where(pick, -jnp.inf, val)
    sel_ref[0] = jnp.transpose(chosen).astype(BF16)


def _overlap_t(n_cmp_pad, n_sel):
    c = np.arange(n_cmp_pad)
    start, end = c * CMP_STRIDE, c * CMP_STRIDE + CMP_LEN - 1
    s0 = np.arange(n_sel) * SEL_BLOCK
    ov = (start[None, :] < s0[:, None] + SEL_BLOCK) & (end[None, :] >= s0[:, None])
    return jnp.asarray(ov.astype(np.float32), dtype=BF16)


def _cmp_attn(qh, kc, vc, gates, bsz, tq):
    m = qh.shape[1]
    t = m // bsz
    nq = t // tq
    n = kc.shape[2]
    n_sel = t // SEL_BLOCK
    top_k = min(SEL_TOPK, n_sel)
    ovt = _overlap_t(n, n_sel)
    return pl.pallas_call(
        functools.partial(_cmp_attn_kernel, tq=tq, n_sel=n_sel, top_k=top_k),
        grid=(bsz, NSA_KV, nq),
        in_specs=[pl.BlockSpec((NSA_HPG, tq, NSA_DH), lambda b, g, i: (g, b * nq + i, 0)),
                  pl.BlockSpec((1, 1, n, NSA_DH), lambda b, g, i: (b, g, 0, 0)),
                  pl.BlockSpec((1, 1, n, NSA_DH), lambda b, g, i: (b, g, 0, 0)),
                  pl.BlockSpec((n_sel, n), lambda b, g, i: (0, 0)),
                  pl.BlockSpec((1, tq, LANE), lambda b, g, i: (g, b * nq + i, 0))],
        out_specs=[pl.BlockSpec((NSA_HPG, tq, NSA_DH), lambda b, g, i: (g, b * nq + i, 0)),
                   pl.BlockSpec((1, tq, n_sel), lambda b, g, i: (g, b * nq + i, 0))],
        out_shape=[jax.ShapeDtypeStruct((NSA_HEADS, m, NSA_DH), F32),
                   jax.ShapeDtypeStruct((NSA_KV, m, n_sel), BF16)],
        compiler_params=_cparams(("parallel", "parallel", "parallel")),
        name="nsa_cmp_topk",
    )(qh, kc, vc, ovt, gates)


def _flash_kernel(*refs, tq, tk, branch, n_back):
    if branch == 1:
        q_ref, k_ref, v_ref, gt_ref, sel_ref, exp_ref, o_ref, m_ref, l_ref, acc_ref = refs
    else:
        q_ref, k_ref, v_ref, gt_ref, o_ref, m_ref, l_ref, acc_ref = refs
    i = pl.program_id(2)
    j = pl.program_id(3)
    nj = pl.num_programs(3)
    rows = NSA_HPG * tq
    if branch == 1:
        jt = j
        live = jt <= (i * tq + tq - 1) // tk
    else:
        jt = (i * tq) // tk - n_back + j
        live = jt >= 0

    @pl.when(j == 0)
    def _():
        m_ref[...] = jnp.full(m_ref.shape, NEG_INF, F32)
        l_ref[...] = jnp.zeros(l_ref.shape, F32)
        acc_ref[...] = jnp.zeros(acc_ref.shape, F32)

    @pl.when(live)
    def _():
        q4 = q_ref[...].reshape(rows, NSA_DH)
        s = _dot_nt(q4, k_ref[0])
        t_tok = i * tq + lax.broadcasted_iota(jnp.int32, (rows, tk), 0) % tq
        kpos = jt * tk + lax.broadcasted_iota(jnp.int32, (rows, tk), 1)
        if branch == 1:
            picked = _dot(sel_ref[0], exp_ref[...])
            picked = jnp.concatenate([picked] * NSA_HPG, axis=0)
            valid = (picked > 0.5) & (kpos <= t_tok)
        else:
            valid = (kpos <= t_tok) & (kpos > t_tok - WINDOW)
        s = jnp.where(valid, s, NEG_INF)
        m_prev = m_ref[...]
        m_new = jnp.maximum(m_prev, jnp.max(s, axis=1, keepdims=True))
        alpha = jnp.exp(m_prev - m_new)
        p = jnp.exp(s - m_new)
        l_ref[...] = alpha * l_ref[...] + jnp.sum(p, axis=1, keepdims=True)
        acc_ref[...] = alpha * acc_ref[...] + _dot(p.astype(BF16), v_ref[0])
        m_ref[...] = m_new

    @pl.when(j == nj - 1)
    def _():
        gates = gt_ref[0]
        out = acc_ref[...] / l_ref[...]
        for h in range(NSA_HPG):
            col = h * N_BRANCH + branch
            o_ref[h] = out[h * tq:(h + 1) * tq] * gates[:, col:col + 1]


def _flash(qh, kh, vh, gates, bsz, tq, tk, branch, sel=None):
    m = qh.shape[1]
    t = m // bsz
    nq, nk = t // tq, t // tk
    rows = NSA_HPG * tq
    if branch == 1:
        nj = nk
        n_back = 0

        def kidx(b, g, i, j):
            return (g, b * nk + jnp.minimum(j, (i * tq + tq - 1) // tk), 0)
    else:
        n_back = -(-(WINDOW - 1) // tk)
        nj = n_back + -(-tq // tk)

        def kidx(b, g, i, j):
            return (g, b * nk + jnp.maximum((i * tq) // tk - n_back + j, 0), 0)

    qspec = pl.BlockSpec((NSA_HPG, tq, NSA_DH), lambda b, g, i, j: (g, b * nq + i, 0))
    in_specs = [qspec, pl.BlockSpec((1, tk, NSA_DH), kidx), pl.BlockSpec((1, tk, NSA_DH), kidx),
                pl.BlockSpec((1, tq, LANE), lambda b, g, i, j: (g, b * nq + i, 0))]
    args = [qh, kh, vh, gates]
    if branch == 1:
        n_sel = t // SEL_BLOCK
        expand = jnp.asarray((np.arange(t)[None, :] // SEL_BLOCK == np.arange(n_sel)[:, None]).astype(np.float32),
                             dtype=BF16)
        in_specs += [pl.BlockSpec((1, tq, n_sel), lambda b, g, i, j: (g, b * nq + i, 0)),
                     pl.BlockSpec((n_sel, tk), lambda b, g, i, j: (0, jnp.minimum(j, (i * tq + tq - 1) // tk)))]
        args += [sel, expand]
    return pl.pallas_call(
        functools.partial(_flash_kernel, tq=tq, tk=tk, branch=branch, n_back=n_back),
        grid=(bsz, NSA_KV, nq, nj),
        in_specs=in_specs,
        out_specs=qspec,
        out_shape=jax.ShapeDtypeStruct((NSA_HEADS, m, NSA_DH), F32),
        scratch_shapes=[pltpu.VMEM((rows, 1), F32), pltpu.VMEM((rows, 1), F32), pltpu.VMEM((rows, NSA_DH), F32)],
        compiler_params=_cparams(("parallel", "parallel", "parallel", "arbitrary")),
        name="nsa_sel_attn" if branch == 1 else "nsa_win_attn",
    )(*args)


def _outproj_kernel(x_ref, ygla_ref, yconv_ref, os5_ref, d_ref, gw_ref, gb_ref, oc_ref, os_ref, ow_ref, sg_ref,
                    wo_ref, o_ref):
    u = os5_ref[:, 0:S5_W]
    sgate = os5_ref[:, S5_W:2 * S5_W]
    y = _gelu_tanh(yconv_ref[...] + d_ref[...] * u)
    z = _dot(y.astype(BF16), gw_ref[...]) + gb_ref[...]
    ys5 = z[:, :S5_W] * _sigmoid(z[:, S5_W:]) * _silu(sgate)
    acc = x_ref[...] + _dot(ygla_ref[...].astype(BF16), wo_ref[0:GLA_W, :])
    acc = acc + _dot(ys5.astype(BF16), wo_ref[GLA_W:GLA_W + S5_W, :])
    base = GLA_W + S5_W
    for h in range(NSA_HEADS):
        yh = (oc_ref[h] + os_ref[h] + ow_ref[h]) * sg_ref[h]
        acc = acc + _dot(yh.astype(BF16), wo_ref[base + h * NSA_DH:base + (h + 1) * NSA_DH, :])
    o_ref[...] = acc


def _outproj(x2, ygla, yconv, os5, s5_d, glu_w, glu_b, oc, osel, ow, sg, w_out, tm):
    m = x2.shape[0]
    tok = lambda w: pl.BlockSpec((tm, w), lambda i: (i, 0))
    head = pl.BlockSpec((NSA_HEADS, tm, NSA_DH), lambda i: (0, i, 0))
    full = lambda a: pl.BlockSpec(a.shape, lambda i: (0,) * a.ndim)
    d = s5_d.reshape(1, S5_W)
    gw = glu_w.astype(BF16)
    gb = glu_b.reshape(1, 2 * S5_W)
    wo = w_out.astype(BF16)
    return pl.pallas_call(
        _outproj_kernel,
        grid=(m // tm,),
        in_specs=[tok(D_MODEL), tok(GLA_W), tok(S5_W), tok(S5_COLS), full(d), full(gw), full(gb),
                  head, head, head, head, full(wo)],
        out_specs=tok(D_MODEL),
        out_shape=jax.ShapeDtypeStruct((m, D_MODEL), F32),
        compiler_params=_cparams(("parallel",)),
        name="outproj",
    )(x2, ygla, yconv, os5, d, gw, gb, oc, osel, ow, sg, wo)


def _pick(t, pref):
    while t % pref:
        pref //= 2
    return pref


def _layer(x2, bsz, p):
    m = x2.shape[0]
    t = m // bsz
    tm = _pick(m, 512)
    ogla, os5, onsa = _inproj(x2, p['norm_g'], _inproj_weight(p['w_in']), tm)

    ygla = _gla(ogla, bsz, p['gla_w2'], p['gla_b2'], p['gla_onorm'], _pick(t, 256))

    n_super = t // S5_L
    ops = _s5_operators(p['s5_lam_re'], p['s5_lam_im'], p['s5_log_step'], p['s5_b_re'], p['s5_b_im'],
                        p['s5_c_re'], p['s5_c_im'], n_super)
    u_t = os5[:, :S5_W].astype(BF16).reshape(bsz, n_super, S5_L, S5_GROUPS, S5_CH)
    u_t = jnp.transpose(u_t, (0, 3, 1, 2, 4)).reshape(bsz, S5_GROUPS, n_super, S5_L * S5_CH)
    yconv = _s5_conv(u_t, ops).reshape(bsz, S5_GROUPS, n_super, S5_L, S5_CH)
    yconv = jnp.transpose(yconv, (0, 2, 3, 1, 4)).reshape(m, S5_W)

    qh, kc_raw, vc_raw, ks, vs, kw, vw, gates, sg = _nsa_prep(onsa, p['nsa_qn'], p['nsa_kn'], p['nsa_gate_b'], tm)
    kc = _compress(kc_raw, bsz, p['nsa_cmp_pos'][0], p['nsa_cmp_w1'][0], p['nsa_cmp_b1'][0],
                   p['nsa_cmp_w2'][0], p['nsa_cmp_b2'][0], p['nsa_kn'][0])
    vc = _compress(vc_raw, bsz, p['nsa_cmp_pos'][1], p['nsa_cmp_w1'][1], p['nsa_cmp_b1'][1],
                   p['nsa_cmp_w2'][1], p['nsa_cmp_b2'][1], None)
    tq = _pick(t, 128)
    oc, sel = _cmp_attn(qh, kc, vc, gates, bsz, tq)
    osel = _flash(qh, ks, vs, gates, bsz, tq, _pick(t, 512), 1, sel)
    ow = _flash(qh, kw, vw, gates, bsz, _pick(t, 256), _pick(t, 256), 2)

    return _outproj(x2, ygla, yconv, os5, p['s5_d'], p['s5_glu_w'], p['s5_glu_b'], oc, osel, ow, sg,
                    p['w_out'], tm)


def kernel(x, norm_g, w_in, gla_w2, gla_b2, gla_onorm, s5_lam_re, s5_lam_im, s5_log_step, s5_b_re, s5_b_im,
           s5_c_re, s5_c_im, s5_d, s5_glu_w, s5_glu_b, nsa_gate_b, nsa_qn, nsa_kn, nsa_cmp_pos, nsa_cmp_w1,
           nsa_cmp_b1, nsa_cmp_w2, nsa_cmp_b2, w_out):
    params = dict(norm_g=norm_g, w_in=w_in, gla_w2=gla_w2, gla_b2=gla_b2, gla_onorm=gla_onorm,
                  s5_lam_re=s5_lam_re, s5_lam_im=s5_lam_im, s5_log_step=s5_log_step, s5_b_re=s5_b_re,
                  s5_b_im=s5_b_im, s5_c_re=s5_c_re, s5_c_im=s5_c_im, s5_d=s5_d, s5_glu_w=s5_glu_w,
                  s5_glu_b=s5_glu_b, nsa_gate_b=nsa_gate_b, nsa_qn=nsa_qn, nsa_kn=nsa_kn,
                  nsa_cmp_pos=nsa_cmp_pos, nsa_cmp_w1=nsa_cmp_w1, nsa_cmp_b1=nsa_cmp_b1,
                  nsa_cmp_w2=nsa_cmp_w2, nsa_cmp_b2=nsa_cmp_b2, w_out=w_out)
    bsz, t, d = x.shape
    x2 = x.reshape(bsz * t, d)
    for layer in range(w_in.shape[0]):
        x2 = _layer(x2, bsz, {k: v[layer] for k, v in params.items()})
    return x2.reshape(bsz, t, d)
```

```python
import functools
import math

import numpy as np
import jax
import jax.numpy as jnp
from jax import lax
from jax.experimental import pallas as pl
from jax.experimental.pallas import tpu as pltpu

F32 = jnp.float32
BF16 = jnp.bfloat16

D_MODEL = 1024
GLA_HEADS, GLA_DK, GLA_DV = 4, 32, 64
GLA_QK = GLA_HEADS * GLA_DK
GLA_W = GLA_HEADS * GLA_DV
GLA_LOWRANK = 16
GLA_GATE_NORM = 16.0
GLA_CHUNK = 32
S5_GROUPS, S5_CH, S5_STATE = 16, 16, 64
S5_W = S5_GROUPS * S5_CH
S5_L = 16
NSA_HEADS, NSA_KV, NSA_DH = 8, 2, 64
NSA_HPG = NSA_HEADS // NSA_KV
NSA_W = NSA_HEADS * NSA_DH
N_BRANCH = 3
CMP_LEN, CMP_STRIDE, CMP_HIDDEN = 32, 16, 256
SEL_BLOCK, SEL_TOPK = 64, 16
WINDOW = 512
RMS_EPS = 1e-6
NEG_INF = -1e30
FORCE_SCORE = 1e4

LANE = 128
VMEM_LIMIT = 56 * 1024 * 1024

GLA_COLS = GLA_QK * 2 + GLA_W * 2 + LANE
S5_COLS = 2 * S5_W
NSA_KV_W = N_BRANCH * 2 * NSA_KV * NSA_DH
NSA_COLS = NSA_W + NSA_KV_W + NSA_W + LANE


def _cparams(sem):
    return pltpu.CompilerParams(dimension_semantics=sem, vmem_limit_bytes=VMEM_LIMIT)


def _split_bf16(x):
    hi = x.astype(BF16)
    lo = (x - hi.astype(F32)).astype(BF16)
    return hi, lo


def _dot(a, b):
    return jnp.dot(a, b, preferred_element_type=F32)


def _dot_nt(a, b):
    return lax.dot_general(a, b, (((1,), (1,)), ((), ())), preferred_element_type=F32)


def _dot_tn(a, b):
    return lax.dot_general(a, b, (((0,), (0,)), ((), ())), preferred_element_type=F32)


def _sigmoid(x):
    return 1.0 / (1.0 + jnp.exp(-x))


def _silu(x):
    return x * _sigmoid(x)


def _gelu_tanh(x):
    c = math.sqrt(2.0 / math.pi)
    return 0.5 * x * (1.0 + jnp.tanh(c * (x + 0.044715 * (x * x * x))))


def _seg_mean_sq(x, avg):
    hi, lo = _split_bf16(x * x)
    return _dot(hi, avg) + _dot(lo, avg)


def _avg_matrix(width, seg):
    idx = np.arange(width) // seg
    return jnp.asarray((idx[:, None] == idx[None, :]).astype(np.float32) / seg, dtype=BF16)


def _inproj_kernel(x_ref, g_ref, w_ref, ogla_ref, os5_ref, onsa_ref):
    x = x_ref[...]
    ms = jnp.mean(x * x, axis=-1, keepdims=True)
    h = (x * lax.rsqrt(ms + RMS_EPS) * g_ref[...]).astype(BF16)
    ogla_ref[...] = _dot(h, w_ref[:, 0:GLA_COLS])
    os5_ref[...] = _dot(h, w_ref[:, GLA_COLS:GLA_COLS + S5_COLS])
    onsa_ref[...] = _dot(h, w_ref[:, GLA_COLS + S5_COLS:])


def _inproj(x2, norm_g, w_all, tm):
    m = x2.shape[0]
    ncols = w_all.shape[1]
    return pl.pallas_call(
        _inproj_kernel,
        grid=(m // tm,),
        in_specs=[pl.BlockSpec((tm, D_MODEL), lambda i: (i, 0)),
                  pl.BlockSpec((1, D_MODEL), lambda i: (0, 0)),
                  pl.BlockSpec((D_MODEL, ncols), lambda i: (0, 0))],
        out_specs=[pl.BlockSpec((tm, GLA_COLS), lambda i: (i, 0)),
                   pl.BlockSpec((tm, S5_COLS), lambda i: (i, 0)),
                   pl.BlockSpec((tm, NSA_COLS), lambda i: (i, 0))],
        out_shape=[jax.ShapeDtypeStruct((m, GLA_COLS), F32),
                   jax.ShapeDtypeStruct((m, S5_COLS), F32),
                   jax.ShapeDtypeStruct((m, NSA_COLS), F32)],
        compiler_params=_cparams(("parallel",)),
        name="inproj",
    )(x2, norm_g.reshape(1, D_MODEL), w_all)


def _inproj_weight(w_in_l):
    sizes = (GLA_QK, GLA_QK, GLA_W, GLA_LOWRANK, GLA_W, S5_W, S5_W,
             NSA_W, NSA_KV_W, NSA_HEADS * N_BRANCH, NSA_W)
    offs = np.concatenate([[0], np.cumsum(sizes)])
    seg = [w_in_l[:, offs[i]:offs[i + 1]] for i in range(len(sizes))]
    gq, gk, gv, glr, gg, su, sg, nq, nkv, ngl, ng = seg

    def pad(a):
        return jnp.pad(a, ((0, 0), (0, LANE - a.shape[1])))

    return jnp.concatenate([gq, gk, gv, gg, pad(glr), su, sg, nq, nkv, ng, pad(ngl)], axis=1).astype(BF16)


GLA_LEVELS = (16, 8, 4, 2, 1)


def _gla_exponent_matrices(tt):
    r = np.arange(tt)[:, None]
    c = np.arange(tt)[None, :]
    same = (r // GLA_CHUNK) == (c // GLA_CHUNK)
    mats = [same & (c <= r), same & (c > r)]
    for s in GLA_LEVELS:
        blk_r = r // s
        odd = (blk_r % 2) == 1
        mats.append(odd & (c >= blk_r * s) & (c <= r))
        mats.append((~odd) & (c > r) & (c <= blk_r * s + s - 1))
    return jnp.asarray(np.stack(mats).astype(np.float32), dtype=BF16)


def _gla_kernel(x_ref, dm_ref, w2_ref, b2_ref, on_ref, avg_ref, y_ref, st_ref, *, tt):
    nchunk = tt // GLA_CHUNK

    @pl.when(pl.program_id(1) == 0)
    def _():
        st_ref[...] = jnp.zeros_like(st_ref)

    x = x_ref[...]
    q = x[:, 0:GLA_QK] * (GLA_DK ** -0.5)
    k = x[:, GLA_QK:2 * GLA_QK]
    v = x[:, 2 * GLA_QK:2 * GLA_QK + GLA_W]
    gate = x[:, 2 * GLA_QK + GLA_W:2 * GLA_QK + 2 * GLA_W]
    lr = x[:, 2 * GLA_QK + 2 * GLA_W:]
    z = _dot(lr.astype(BF16), w2_ref[...]) + b2_ref[...]
    glog = -(jnp.maximum(-z, 0.0) + jnp.log1p(jnp.exp(-jnp.abs(z)))) / GLA_GATE_NORM

    ghi, glo = _split_bf16(glog)
    gcat = jnp.concatenate([ghi, glo], axis=1)

    def expo(i):
        e = _dot(dm_ref[i], gcat)
        return e[:, :GLA_QK] + e[:, GLA_QK:]

    row = lax.broadcasted_iota(jnp.int32, (tt, GLA_QK), 0)
    lane = lax.broadcasted_iota(jnp.int32, (tt, GLA_QK), 1)
    rr = lax.broadcasted_iota(jnp.int32, (GLA_HEADS * tt, tt), 0) % tt
    cc = lax.broadcasted_iota(jnp.int32, (GLA_HEADS * tt, tt), 1)
    head_masks = [(lane // GLA_DK) == h for h in range(GLA_HEADS)]

    def stack_heads(a):
        return jnp.concatenate([jnp.where(m, a, 0.0) for m in head_masks], axis=0).astype(BF16)

    att = jnp.where(rr == cc, _dot_nt(stack_heads(q), k.astype(BF16)), 0.0)
    for li, s in enumerate(GLA_LEVELS):
        odd = ((row // s) % 2) == 1
        qs = jnp.where(odd, q * jnp.exp(expo(2 + 2 * li)), 0.0)
        ks = jnp.where(odd, 0.0, k * jnp.exp(expo(3 + 2 * li)))
        blk = (rr // (2 * s)) == (cc // (2 * s))
        att = att + jnp.where(blk, _dot_nt(stack_heads(qs), ks.astype(BF16)), 0.0)
    att = att.astype(BF16)

    lane_v = lax.broadcasted_iota(jnp.int32, (tt, GLA_W), 1)
    o = jnp.zeros((tt, GLA_W), F32)
    for h in range(GLA_HEADS):
        vh = jnp.where((lane_v // GLA_DV) == h, v, 0.0).astype(BF16)
        o = o + _dot(att[h * tt:(h + 1) * tt], vh)

    bcum = expo(0)
    brev = expo(1)
    qe = (q * jnp.exp(bcum)).astype(BF16)
    ke = (k * jnp.exp(brev)).astype(BF16)
    vb = v.astype(BF16)
    sr = lax.broadcasted_iota(jnp.int32, (GLA_W, GLA_QK), 0)
    sc = lax.broadcasted_iota(jnp.int32, (GLA_W, GLA_QK), 1)
    st_mask = (sr // GLA_DV) == (sc // GLA_DK)
    st = st_ref[...]
    o_inter = []
    for c in range(nchunk):
        lo, hi = c * GLA_CHUNK, (c + 1) * GLA_CHUNK
        o_inter.append(_dot_nt(qe[lo:hi], st.astype(BF16)))
        kv_t = jnp.where(st_mask, _dot_tn(vb[lo:hi], ke[lo:hi]), 0.0)
        st = st * jnp.exp(bcum[hi - 1:hi, :]) + kv_t
    st_ref[...] = st
    o = o + jnp.concatenate(o_inter, axis=0)

    ms = _seg_mean_sq(o, avg_ref[...])
    y_ref[...] = o * lax.rsqrt(ms + RMS_EPS) * on_ref[...] * _silu(gate)


def _gla(ogla, bsz, w2, b2, onorm, tt):
    m = ogla.shape[0]
    nt = m // bsz // tt
    w2p = jnp.pad(w2, ((0, LANE - GLA_LOWRANK), (0, 0))).astype(BF16)
    dmats = _gla_exponent_matrices(tt)
    return pl.pallas_call(
        functools.partial(_gla_kernel, tt=tt),
        grid=(bsz, nt),
        in_specs=[pl.BlockSpec((tt, GLA_COLS), lambda b, t: (b * nt + t, 0)),
                  pl.BlockSpec(dmats.shape, lambda b, t: (0, 0, 0)),
                  pl.BlockSpec((LANE, GLA_QK), lambda b, t: (0, 0)),
                  pl.BlockSpec((1, GLA_QK), lambda b, t: (0, 0)),
                  pl.BlockSpec((1, GLA_W), lambda b, t: (0, 0)),
                  pl.BlockSpec((GLA_W, GLA_W), lambda b, t: (0, 0))],
        out_specs=pl.BlockSpec((tt, GLA_W), lambda b, t: (b * nt + t, 0)),
        out_shape=jax.ShapeDtypeStruct((m, GLA_W), F32),
        scratch_shapes=[pltpu.VMEM((GLA_W, GLA_QK), F32)],
        compiler_params=_cparams(("arbitrary", "arbitrary")),
        name="gla",
    )(ogla, dmats, w2p, b2.reshape(1, GLA_QK), jnp.tile(onorm, GLA_HEADS).reshape(1, GLA_W),
      _avg_matrix(GLA_W, GLA_DV))


def _s5_operators(lam_re, lam_im, log_step, b_re, b_im, c_re, c_im, n_super):
    lr = jnp.minimum(lam_re.astype(F32), -1e-4)
    li = lam_im.astype(F32)
    dt = jnp.exp(log_step.astype(F32))[:, None]

    def apow(tau):
        tau = jnp.asarray(tau, F32)[..., None, None]
        mag = jnp.exp(lr * dt * tau)
        return mag * jnp.cos(li * dt * tau), mag * jnp.sin(li * dt * tau)

    ar, ai = apow(1.0)
    den = lr * lr + li * li
    fr = ((ar - 1.0) * lr + ai * li) / den
    fi = (ai * lr - (ar - 1.0) * li) / den
    br, bi = b_re.astype(F32), b_im.astype(F32)
    bbr = fr[..., None] * br - fi[..., None] * bi
    bbi = fr[..., None] * bi + fi[..., None] * br
    cr, ci = c_re.astype(F32), c_im.astype(F32)
    L = S5_L
    pr, pi = apow(np.arange(L + 1))
    hp = lax.Precision.HIGHEST
    cpr = cr[None] * jnp.swapaxes(pr, 1, 1)[:, :, None, :] - ci[None] * pi[:, :, None, :]
    cpi = cr[None] * pi[:, :, None, :] + ci[None] * pr[:, :, None, :]
    taps = (jnp.einsum('tgcp,gpd->tgcd', cpr[:L], bbr, precision=hp)
            - jnp.einsum('tgcp,gpd->tgcd', cpi[:L], bbi, precision=hp))
    s_idx = np.arange(L)
    lag = s_idx[None, :] - s_idx[:, None]
    kin = jnp.where((lag >= 0)[:, :, None, None, None], taps[np.clip(lag, 0, L - 1)], 0.0)
    kin = jnp.transpose(kin, (2, 0, 4, 1, 3)).reshape(S5_GROUPS, L * S5_CH, L * S5_CH)
    rev = pr[L - 1 - s_idx], pi[L - 1 - s_idx]
    vre = rev[0][..., None] * bbr[None] - rev[1][..., None] * bbi[None]
    vim = rev[0][..., None] * bbi[None] + rev[1][..., None] * bbr[None]
    vre = jnp.transpose(vre, (1, 0, 3, 2)).reshape(S5_GROUPS, L * S5_CH, S5_STATE)
    vim = jnp.transpose(vim, (1, 0, 3, 2)).reshape(S5_GROUPS, L * S5_CH, S5_STATE)
    wre = jnp.transpose(cpr[1:], (1, 3, 0, 2)).reshape(S5_GROUPS, S5_STATE, L * S5_CH)
    wim = -jnp.transpose(cpi[1:], (1, 3, 0, 2)).reshape(S5_GROUPS, S5_STATE, L * S5_CH)

    def pair_diag(a):
        g, kk, nn = a.shape
        a = a.reshape(g // 2, 2, kk, nn)
        zero = jnp.zeros_like(a[:, 0])
        top = jnp.concatenate([a[:, 0], zero], axis=2)
        bot = jnp.concatenate([zero, a[:, 1]], axis=2)
        return jnp.concatenate([top, bot], axis=1)

    nsteps = max(1, int(math.ceil(math.log2(n_super))))
    spr, spi = apow(float(L) * (2.0 ** np.arange(nsteps)))
    spr = spr.reshape(nsteps, 1, S5_GROUPS * S5_STATE)
    spi = spi.reshape(nsteps, 1, S5_GROUPS * S5_STATE)
    return (kin.astype(BF16), pair_diag(vre).astype(BF16), pair_diag(vim).astype(BF16),
            pair_diag(wre).astype(BF16), pair_diag(wim).astype(BF16), spr, spi)


def _s5_kernel(u_ref, kin_ref, vre_ref, vim_ref, wre_ref, wim_ref, spr_ref, spi_ref, y_ref, xr_ref, xi_ref,
               *, n_super, nsteps):
    npair = S5_GROUPS // 2
    for m in range(npair):
        up = jnp.concatenate([u_ref[0, 2 * m], u_ref[0, 2 * m + 1]], axis=1)
        xr_ref[:, m * LANE:(m + 1) * LANE] = _dot(up, vre_ref[m])
        xi_ref[:, m * LANE:(m + 1) * LANE] = _dot(up, vim_ref[m])
    xr = xr_ref[...]
    xi = xi_ref[...]
    row = lax.broadcasted_iota(jnp.int32, xr.shape, 0)
    for kstep in range(nsteps):
        d = 1 << kstep
        sr = jnp.where(row >= d, pltpu.roll(xr, d, 0), 0.0)
        si = jnp.where(row >= d, pltpu.roll(xi, d, 0), 0.0)
        pr = spr_ref[kstep]
        pi = spi_ref[kstep]
        xr, xi = xr + pr * sr - pi * si, xi + pr * si + pi * sr
    er = jnp.where(row >= 1, pltpu.roll(xr, 1, 0), 0.0).astype(BF16)
    ei = jnp.where(row >= 1, pltpu.roll(xi, 1, 0), 0.0).astype(BF16)
    width = S5_L * S5_CH
    for m in range(npair):
        inter = (_dot(er[:, m * LANE:(m + 1) * LANE], wre_ref[m])
                 + _dot(ei[:, m * LANE:(m + 1) * LANE], wim_ref[m]))
        for j in range(2):
            g = 2 * m + j
            y_ref[0, g] = _dot(u_ref[0, g], kin_ref[g]) + inter[:, j * width:(j + 1) * width]


def _s5_conv(u_t, ops):
    bsz, _, n_super, width = u_t.shape
    kin, vre, vim, wre, wim, spr, spi = ops
    nsteps = spr.shape[0]
    nst = S5_GROUPS * S5_STATE

    def full(a):
        nd = a.ndim
        return pl.BlockSpec(a.shape, lambda b: (0,) * nd)

    return pl.pallas_call(
        functools.partial(_s5_kernel, n_super=n_super, nsteps=nsteps),
        grid=(bsz,),
        in_specs=[pl.BlockSpec((1, S5_GROUPS, n_super, width), lambda b: (b, 0, 0, 0)),
                  full(kin), full(vre), full(vim), full(wre), full(wim), full(spr), full(spi)],
        out_specs=pl.BlockSpec((1, S5_GROUPS, n_super, width), lambda b: (b, 0, 0, 0)),
        out_shape=jax.ShapeDtypeStruct((bsz, S5_GROUPS, n_super, width), F32),
        scratch_shapes=[pltpu.VMEM((n_super, nst), F32), pltpu.VMEM((n_super, nst), F32)],
        compiler_params=_cparams(("arbitrary",)),
        name="s5_conv",
    )(u_t, kin, vre, vim, wre, wim, spr, spi)


def _nsa_prep_kernel(x_ref, qg_ref, kg_ref, gb_ref, avgq_ref, avgk_ref,
                     q_ref, kc_ref, vc_ref, ks_ref, vs_ref, kw_ref, vw_ref, gt_ref, sg_ref):
    x = x_ref[...]
    q = x[:, 0:NSA_W]
    qn = q * lax.rsqrt(_seg_mean_sq(q, avgq_ref[...]) + RMS_EPS) * qg_ref[...] * (NSA_DH ** -0.5)
    for h in range(NSA_HEADS):
        q_ref[h] = qn[:, h * NSA_DH:(h + 1) * NSA_DH].astype(BF16)
    kv = x[:, NSA_W:NSA_W + NSA_KV_W]
    gw = NSA_KV * NSA_DH
    kc_ref[...] = kv[:, 0:gw]
    vc_ref[...] = kv[:, gw:2 * gw]

    def knorm(a, br):
        return a * lax.rsqrt(_seg_mean_sq(a, avgk_ref[...]) + RMS_EPS) * kg_ref[br]

    ksel = knorm(kv[:, 2 * gw:3 * gw], 1)
    vsel = kv[:, 3 * gw:4 * gw]
    kwin = knorm(kv[:, 4 * gw:5 * gw], 2)
    vwin = kv[:, 5 * gw:6 * gw]
    low = lax.broadcasted_iota(jnp.int32, vsel.shape, 1) < NSA_DH

    def with_ones(a, g):
        return jnp.where(low, a if g == 0 else pltpu.roll(a, NSA_DH, 1), 1.0).astype(BF16)

    for g in range(NSA_KV):
        sl = slice(g * NSA_DH, (g + 1) * NSA_DH)
        ks_ref[g] = ksel[:, sl].astype(BF16)
        vs_ref[g] = with_ones(vsel, g)
        kw_ref[g] = kwin[:, sl].astype(BF16)
        vw_ref[g] = with_ones(vwin, g)
    gate = x[:, NSA_W + NSA_KV_W:2 * NSA_W + NSA_KV_W]
    sgate = _silu(gate)
    for h in range(NSA_HEADS):
        sg_ref[h] = sgate[:, h * NSA_DH:(h + 1) * NSA_DH]
    gl = x[:, 2 * NSA_W + NSA_KV_W:]
    sig = _sigmoid(gl + gb_ref[...])
    per_group = NSA_HPG * N_BRANCH
    gt_ref[0] = sig
    gt_ref[1] = pltpu.roll(sig, LANE - per_group, 1)


def _nsa_prep(onsa, qn_g, kn_g, gate_b, tm):
    m = onsa.shape[0]
    gw = NSA_KV * NSA_DH
    head = lambda n: pl.BlockSpec((n, tm, NSA_DH), lambda i: (0, i, 0))
    wide = pl.BlockSpec((NSA_KV, tm, LANE), lambda i: (0, i, 0))
    tok = lambda w: pl.BlockSpec((tm, w), lambda i: (i, 0))
    gbp = jnp.pad(gate_b, (0, LANE - gate_b.shape[0])).reshape(1, LANE)
    return pl.pallas_call(
        _nsa_prep_kernel,
        grid=(m // tm,),
        in_specs=[pl.BlockSpec((tm, NSA_COLS), lambda i: (i, 0)),
                  pl.BlockSpec((1, NSA_W), lambda i: (0, 0)),
                  pl.BlockSpec((N_BRANCH, 1, gw), lambda i: (0, 0, 0)),
                  pl.BlockSpec((1, LANE), lambda i: (0, 0)),
                  pl.BlockSpec((NSA_W, NSA_W), lambda i: (0, 0)),
                  pl.BlockSpec((gw, gw), lambda i: (0, 0))],
        out_specs=[head(NSA_HEADS), tok(gw), tok(gw), head(NSA_KV), wide, head(NSA_KV), wide,
                   wide, head(NSA_HEADS)],
        out_shape=[jax.ShapeDtypeStruct((NSA_HEADS, m, NSA_DH), BF16),
                   jax.ShapeDtypeStruct((m, gw), F32), jax.ShapeDtypeStruct((m, gw), F32),
                   jax.ShapeDtypeStruct((NSA_KV, m, NSA_DH), BF16), jax.ShapeDtypeStruct((NSA_KV, m, LANE), BF16),
                   jax.ShapeDtypeStruct((NSA_KV, m, NSA_DH), BF16), jax.ShapeDtypeStruct((NSA_KV, m, LANE), BF16),
                   jax.ShapeDtypeStruct((NSA_KV, m, LANE), F32),
                   jax.ShapeDtypeStruct((NSA_HEADS, m, NSA_DH), F32)],
        compiler_params=_cparams(("parallel",)),
        name="nsa_prep",
    )(onsa, jnp.tile(qn_g, NSA_HEADS).reshape(1, NSA_W),
      jnp.tile(kn_g, (1, NSA_KV)).reshape(N_BRANCH, 1, gw), gbp,
      _avg_matrix(NSA_W, NSA_DH), _avg_matrix(gw, NSA_DH))


def _compress_kernel(a_ref, pt_ref, pb_ref, w1t_ref, w1b_ref, b1_ref, w2_ref, b2_ref, ng_ref, avg_ref, o_ref,
                     *, normalise):
    a = a_ref[0]
    n = a.shape[0]
    h1 = _dot((a + pt_ref[...]).astype(BF16), w1t_ref[...])
    h2 = _dot((a + pb_ref[...]).astype(BF16), w1b_ref[...])
    hid = _gelu_tanh(h1 + pltpu.roll(h2, n - 1, 0) + b1_ref[...])
    out = _dot(hid.astype(BF16), w2_ref[...]) + b2_ref[...]
    if normalise:
        out = out * lax.rsqrt(_seg_mean_sq(out, avg_ref[...]) + RMS_EPS) * ng_ref[...]
    for g in range(NSA_KV):
        o_ref[0, g] = out[:, g * NSA_DH:(g + 1) * NSA_DH].astype(BF16)


def _compress(raw, bsz, pos, w1, b1, w2, b2, norm_g):
    gw = NSA_KV * NSA_DH
    n = raw.shape[0] // bsz // CMP_STRIDE
    a2 = raw.reshape(bsz, n, CMP_STRIDE * gw)
    half = CMP_STRIDE * NSA_DH

    def pos_rows(p):
        return jnp.tile(p[:, None, :], (1, NSA_KV, 1)).reshape(1, CMP_STRIDE * gw)

    def w1_rows(w):
        w = w.reshape(CMP_STRIDE, NSA_DH, CMP_HIDDEN)
        z = jnp.zeros_like(w)
        g0 = jnp.concatenate([w, z], axis=2)
        g1 = jnp.concatenate([z, w], axis=2)
        return jnp.stack([g0, g1], axis=1).reshape(CMP_STRIDE * gw, NSA_KV * CMP_HIDDEN).astype(BF16)

    zero = jnp.zeros_like(w2)
    w2b = jnp.concatenate([jnp.concatenate([w2, zero], axis=1), jnp.concatenate([zero, w2], axis=1)], axis=0)
    normalise = norm_g is not None
    ng = jnp.tile(norm_g if normalise else jnp.ones((NSA_DH,), F32), NSA_KV).reshape(1, gw)
    full = lambda a: pl.BlockSpec(a.shape, lambda b: (0,) * a.ndim)
    args = (pos_rows(pos[:CMP_STRIDE]), pos_rows(pos[CMP_STRIDE:]), w1_rows(w1[:half]), w1_rows(w1[half:]),
            jnp.tile(b1, NSA_KV).reshape(1, -1), w2b.astype(BF16), jnp.tile(b2, NSA_KV).reshape(1, gw), ng,
            _avg_matrix(gw, NSA_DH))
    return pl.pallas_call(
        functools.partial(_compress_kernel, normalise=normalise),
        grid=(bsz,),
        in_specs=[pl.BlockSpec((1, n, CMP_STRIDE * gw), lambda b: (b, 0, 0))] + [full(a) for a in args],
        out_specs=pl.BlockSpec((1, NSA_KV, n, NSA_DH), lambda b: (b, 0, 0, 0)),
        out_shape=jax.ShapeDtypeStruct((bsz, NSA_KV, n, NSA_DH), BF16),
        compiler_params=_cparams(("parallel",)),
        name="nsa_compress",
    )(a2, *args)


def _lane_slabs(a):
    return [a[:, c * LANE:(c + 1) * LANE] for c in range(a.shape[1] // LANE)]


def _row_max(slabs):
    mx = functools.reduce(jnp.maximum, slabs)
    return jnp.broadcast_to(jnp.max(mx, axis=1, keepdims=True), mx.shape)


def _nsa_attn_kernel(q_ref, kc_ref, vc_ref, ovt_ref, ks_ref, vs_ref, exp_ref, kw_ref, vw_ref, gt_ref, sg_ref,
                     o_ref, m_ref, acc_ref, p_ref, *, tq, tk, wlen, n_sel, top_k):
    i = pl.program_id(2)
    rows = NSA_HPG * tq
    q4 = q_ref[...].reshape(rows, NSA_DH)
    gates = gt_ref[0]

    kc = kc_ref[0, 0]
    n = kc.shape[0]
    s = _dot_nt(q4, kc)
    t_tok = i * tq + lax.broadcasted_iota(jnp.int32, (tq, n), 0)
    c_idx = lax.broadcasted_iota(jnp.int32, (tq, n), 1)
    mask = (c_idx * CMP_STRIDE + (CMP_LEN - 1)) <= t_tok
    real = c_idx < (n - 1)
    psum = jnp.zeros((tq, n), F32)
    for h in range(NSA_HPG):
        sm = jnp.where(mask, s[h * tq:(h + 1) * tq], NEG_INF)
        e = jnp.exp(sm - jnp.max(sm, axis=1, keepdims=True))
        denom = jnp.sum(jnp.where(real, e, 0.0), axis=1, keepdims=True)
        p = jnp.where(mask, e, 0.0) / denom
        psum = psum + p
        p_ref[h * tq:(h + 1) * tq, 0:n] = p.astype(BF16)
    o_cmp = _dot(p_ref[:, 0:n], vc_ref[0, 0])
    imp = _dot_nt(ovt_ref[...], psum.astype(BF16))

    sid = lax.broadcasted_iota(jnp.int32, imp.shape, 0)
    cur = (i * tq + lax.broadcasted_iota(jnp.int32, imp.shape, 1)) // SEL_BLOCK
    forced = (sid == 0) | (sid == cur) | (sid == cur - 1)
    val = jnp.where(forced, FORCE_SCORE, imp)
    val = jnp.where(sid <= cur, val, NEG_INF)
    chosen = jnp.zeros(imp.shape, F32)
    sidf = sid.astype(F32)
    for _ in range(top_k):
        mval = jnp.max(val, axis=0, keepdims=True)
        first = jnp.min(jnp.where(val == mval, sidf, float(n_sel)), axis=0, keepdims=True)
        pick = sidf == first
        chosen = jnp.where(pick & (mval > 0.5 * NEG_INF), 1.0, chosen)
        val = jnp.where(pick, -jnp.inf, val)
    selb = jnp.where(jnp.transpose(chosen) > 0.5, 0.0, NEG_INF).astype(BF16)

    m_ref[...] = jnp.full(m_ref.shape, NEG_INF, F32)
    acc_ref[...] = jnp.zeros(acc_ref.shape, F32)

    def sel_chunk(j, diagonal):
        koff = pl.multiple_of(j * tk, tk)
        sc = _dot_nt(q4, ks_ref[0, pl.ds(koff, tk), :])
        bias = _dot(selb, exp_ref[j])
        if diagonal:
            tt = i * tq + lax.broadcasted_iota(jnp.int32, (tq, tk), 0)
            kpos = koff + lax.broadcasted_iota(jnp.int32, (tq, tk), 1)
            bias = jnp.where(kpos <= tt, bias, NEG_INF)
        bias_slabs = _lane_slabs(bias)
        for h in range(NSA_HPG):
            r = slice(h * tq, (h + 1) * tq)
            slabs = [a + b for a, b in zip(_lane_slabs(sc[r]), bias_slabs)]
            m_prev = m_ref[r]
            m_new = jnp.maximum(m_prev, _row_max(slabs))
            for c, sl in enumerate(slabs):
                p_ref[r, c * LANE:(c + 1) * LANE] = jnp.exp(sl - m_new).astype(BF16)
            acc_ref[r] = jnp.exp(m_prev - m_new) * acc_ref[r]
            m_ref[r] = m_new
        acc_ref[...] += _dot(p_ref[:, 0:tk], vs_ref[0, pl.ds(koff, tk), :])

    n_full = (i * tq) // tk

    def full_body(j, carry):
        sel_chunk(j, False)
        return carry

    lax.fori_loop(0, n_full, full_body, 0)
    sel_chunk(n_full, True)
    acc_sel = acc_ref[...]
    o_sel = acc_sel[:, :NSA_DH] / pltpu.roll(acc_sel, NSA_DH, 1)[:, :NSA_DH]

    start = pl.multiple_of(jnp.maximum(i * tq + tq - wlen, 0), tq)
    sw = _dot_nt(q4, kw_ref[0, pl.ds(start, wlen), :])
    tt = i * tq + lax.broadcasted_iota(jnp.int32, (tq, wlen), 0)
    kpos = start + lax.broadcasted_iota(jnp.int32, (tq, wlen), 1)
    wbias = _lane_slabs(jnp.where((kpos <= tt) & (kpos > tt - WINDOW), 0.0, NEG_INF))
    for h in range(NSA_HPG):
        r = slice(h * tq, (h + 1) * tq)
        slabs = [a + b for a, b in zip(_lane_slabs(sw[r]), wbias)]
        m_w = _row_max(slabs)
        for c, sl in enumerate(slabs):
            p_ref[r, c * LANE:(c + 1) * LANE] = jnp.exp(sl - m_w).astype(BF16)
    acc_w = _dot(p_ref[:, 0:wlen], vw_ref[0, pl.ds(start, wlen), :])
    o_win = acc_w[:, :NSA_DH] / pltpu.roll(acc_w, NSA_DH, 1)[:, :NSA_DH]

    for h in range(NSA_HPG):
        r = slice(h * tq, (h + 1) * tq)
        c0 = h * N_BRANCH
        y = (gates[:, c0:c0 + 1] * o_cmp[r] + gates[:, c0 + 1:c0 + 2] * o_sel[r]
             + gates[:, c0 + 2:c0 + 3] * o_win[r])
        o_ref[h] = (y * sg_ref[h]).astype(BF16)


def _overlap_t(n_cmp_pad, n_sel):
    c = np.arange(n_cmp_pad)
    start, end = c * CMP_STRIDE, c * CMP_STRIDE + CMP_LEN - 1
    s0 = np.arange(n_sel) * SEL_BLOCK
    ov = (start[None, :] < s0[:, None] + SEL_BLOCK) & (end[None, :] >= s0[:, None])
    return jnp.asarray(ov.astype(np.float32), dtype=BF16)


def _nsa_attn(qh, kc, vc, ks, vs, kw, vw, gates, sg, bsz, tq, tk):
    m = qh.shape[1]
    t = m // bsz
    nq = t // tq
    n = kc.shape[2]
    n_sel = t // SEL_BLOCK
    top_k = min(SEL_TOPK, n_sel)
    wlen = WINDOW + tq
    assert t % tk == 0 and tk % tq == 0 and WINDOW % tq == 0 and t >= wlen and tk % LANE == 0
    ovt = _overlap_t(n, n_sel)
    key_blk = (np.arange(t) // SEL_BLOCK).reshape(t // tk, 1, tk)
    expand = jnp.asarray((key_blk == np.arange(n_sel)[None, :, None]).astype(np.float32), dtype=BF16)
    rows = NSA_HPG * tq
    heads = pl.BlockSpec((NSA_HPG, tq, NSA_DH), lambda b, g, i: (g, b * nq + i, 0))
    cmp_spec = pl.BlockSpec((1, 1, n, NSA_DH), lambda b, g, i: (b, g, 0, 0))
    seq = lambda w: pl.BlockSpec((1, t, w), lambda b, g, i: (g, b, 0))
    return pl.pallas_call(
        functools.partial(_nsa_attn_kernel, tq=tq, tk=tk, wlen=wlen, n_sel=n_sel, top_k=top_k),
        grid=(bsz, NSA_KV, nq),
        in_specs=[heads, cmp_spec, cmp_spec,
                  pl.BlockSpec((n_sel, n), lambda b, g, i: (0, 0)),
                  seq(NSA_DH), seq(LANE),
                  pl.BlockSpec(expand.shape, lambda b, g, i: (0, 0, 0)),
                  seq(NSA_DH), seq(LANE),
                  pl.BlockSpec((1, tq, LANE), lambda b, g, i: (g, b * nq + i, 0)),
                  heads],
        out_specs=heads,
        out_shape=jax.ShapeDtypeStruct((NSA_HEADS, m, NSA_DH), BF16),
        scratch_shapes=[pltpu.VMEM((rows, LANE), F32), pltpu.VMEM((rows, LANE), F32),
                        pltpu.VMEM((rows, max(n, tk, wlen)), BF16)],
        compiler_params=_cparams(("parallel", "parallel", "arbitrary")),
        name="nsa_attn",
    )(qh, kc, vc, ovt, ks, vs, expand, kw, vw, gates, sg)


def _outproj_kernel(x_ref, ygla_ref, yconv_ref, os5_ref, d_ref, gw_ref, gb_ref, ynsa_ref, wo_ref, o_ref):
    u = os5_ref[:, 0:S5_W]
    sgate = os5_ref[:, S5_W:2 * S5_W]
    y = _gelu_tanh(yconv_ref[...] + d_ref[...] * u)
    z = _dot(y.astype(BF16), gw_ref[...]) + gb_ref[...]
    ys5 = z[:, :S5_W] * _sigmoid(z[:, S5_W:]) * _silu(sgate)
    acc = x_ref[...] + _dot(ygla_ref[...].astype(BF16), wo_ref[0:GLA_W, :])
    acc = acc + _dot(ys5.astype(BF16), wo_ref[GLA_W:GLA_W + S5_W, :])
    base = GLA_W + S5_W
    for h in range(NSA_HEADS):
        acc = acc + _dot(ynsa_ref[h], wo_ref[base + h * NSA_DH:base + (h + 1) * NSA_DH, :])
    o_ref[...] = acc


def _outproj(x2, ygla, yconv, os5, s5_d, glu_w, glu_b, ynsa, w_out, tm):
    m = x2.shape[0]
    tok = lambda w: pl.BlockSpec((tm, w), lambda i: (i, 0))
    head = pl.BlockSpec((NSA_HEADS, tm, NSA_DH), lambda i: (0, i, 0))
    full = lambda a: pl.BlockSpec(a.shape, lambda i: (0,) * a.ndim)
    d = s5_d.reshape(1, S5_W)
    gw = glu_w.astype(BF16)
    gb = glu_b.reshape(1, 2 * S5_W)
    wo = w_out.astype(BF16)
    return pl.pallas_call(
        _outproj_kernel,
        grid=(m // tm,),
        in_specs=[tok(D_MODEL), tok(GLA_W), tok(S5_W), tok(S5_COLS), full(d), full(gw), full(gb),
                  head, full(wo)],
        out_specs=tok(D_MODEL),
        out_shape=jax.ShapeDtypeStruct((m, D_MODEL), F32),
        compiler_params=_cparams(("parallel",)),
        name="outproj",
    )(x2, ygla, yconv, os5, d, gw, gb, ynsa, wo)


def _pick(t, pref):
    while t % pref:
        pref //= 2
    return pref


def _layer(x2, bsz, p):
    m = x2.shape[0]
    t = m // bsz
    tm = _pick(m, 512)
    ogla, os5, onsa = _inproj(x2, p['norm_g'], _inproj_weight(p['w_in']), tm)

    ygla = _gla(ogla, bsz, p['gla_w2'], p['gla_b2'], p['gla_onorm'], _pick(t, 256))

    n_super = t // S5_L
    ops = _s5_operators(p['s5_lam_re'], p['s5_lam_im'], p['s5_log_step'], p['s5_b_re'], p['s5_b_im'],
                        p['s5_c_re'], p['s5_c_im'], n_super)
    u_t = os5[:, :S5_W].astype(BF16).reshape(bsz, n_super, S5_L, S5_GROUPS, S5_CH)
    u_t = jnp.transpose(u_t, (0, 3, 1, 2, 4)).reshape(bsz, S5_GROUPS, n_super, S5_L * S5_CH)
    yconv = _s5_conv(u_t, ops).reshape(bsz, S5_GROUPS, n_super, S5_L, S5_CH)
    yconv = jnp.transpose(yconv, (0, 2, 3, 1, 4)).reshape(m, S5_W)

    qh, kc_raw, vc_raw, ks, vs, kw, vw, gates, sg = _nsa_prep(onsa, p['nsa_qn'], p['nsa_kn'], p['nsa_gate_b'], tm)
    kc = _compress(kc_raw, bsz, p['nsa_cmp_pos'][0], p['nsa_cmp_w1'][0], p['nsa_cmp_b1'][0],
                   p['nsa_cmp_w2'][0], p['nsa_cmp_b2'][0], p['nsa_kn'][0])
    vc = _compress(vc_raw, bsz, p['nsa_cmp_pos'][1], p['nsa_cmp_w1'][1], p['nsa_cmp_b1'][1],
                   p['nsa_cmp_w2'][1], p['nsa_cmp_b2'][1], None)
    ynsa = _nsa_attn(qh, kc, vc, ks, vs, kw, vw, gates, sg, bsz, 128, 512)

    return _outproj(x2, ygla, yconv, os5, p['s5_d'], p['s5_glu_w'], p['s5_glu_b'], ynsa, p['w_out'], tm)


def kernel(x, norm_g, w_in, gla_w2, gla_b2, gla_onorm, s5_lam_re, s5_lam_im, s5_log_step, s5_b_re, s5_b_im,
           s5_c_re, s5_c_im, s5_d, s5_glu_w, s5_glu_b, nsa_gate_b, nsa_qn, nsa_kn, nsa_cmp_pos, nsa_cmp_w1,
           nsa_cmp_b1, nsa_cmp_w2, nsa_cmp_b2, w_out):
    params = dict(norm_g=norm_g, w_in=w_in, gla_w2=gla_w2, gla_b2=gla_b2, gla_onorm=gla_onorm,
                  s5_lam_re=s5_lam_re, s5_lam_im=s5_lam_im, s5_log_step=s5_log_step, s5_b_re=s5_b_re,
                  s5_b_im=s5_b_im, s5_c_re=s5_c_re, s5_c_im=s5_c_im, s5_d=s5_d, s5_glu_w=s5_glu_w,
                  s5_glu_b=s5_glu_b, nsa_gate_b=nsa_gate_b, nsa_qn=nsa_qn, nsa_kn=nsa_kn,
                  nsa_cmp_pos=nsa_cmp_pos, nsa_cmp_w1=nsa_cmp_w1, nsa_cmp_b1=nsa_cmp_b1,
                  nsa_cmp_w2=nsa_cmp_w2, nsa_cmp_b2=nsa_cmp_b2, w_out=w_out)
    bsz, t, d = x.shape
    x2 = x.reshape(bsz * t, d)
    for layer in range(w_in.shape[0]):
        x2 = _layer(x2, bsz, {k: v[layer] for k, v in params.items()})
    return x2.reshape(bsz, t, d)
```

```python
import functools
import math

import numpy as np
import jax
import jax.numpy as jnp
from jax import lax
from jax.experimental import pallas as pl
from jax.experimental.pallas import tpu as pltpu

F32 = jnp.float32
BF16 = jnp.bfloat16

D_MODEL = 1024
GLA_HEADS, GLA_DK, GLA_DV = 4, 32, 64
GLA_QK = GLA_HEADS * GLA_DK
GLA_W = GLA_HEADS * GLA_DV
GLA_LOWRANK = 16
GLA_GATE_NORM = 16.0
GLA_CHUNK = 32
S5_GROUPS, S5_CH, S5_STATE = 16, 16, 64
S5_W = S5_GROUPS * S5_CH
S5_L = 16
NSA_HEADS, NSA_KV, NSA_DH = 8, 2, 64
NSA_HPG = NSA_HEADS // NSA_KV
NSA_W = NSA_HEADS * NSA_DH
N_BRANCH = 3
CMP_LEN, CMP_STRIDE, CMP_HIDDEN = 32, 16, 256
SEL_BLOCK, SEL_TOPK = 64, 16
WINDOW = 512
RMS_EPS = 1e-6
NEG_INF = -1e30
FORCE_SCORE = 1e4

LANE = 128
VMEM_LIMIT = 56 * 1024 * 1024

GLA_COLS = GLA_QK * 2 + GLA_W * 2 + LANE
S5_COLS = 2 * S5_W
NSA_KV_W = N_BRANCH * 2 * NSA_KV * NSA_DH
NSA_COLS = NSA_W + NSA_KV_W + NSA_W + LANE


def _cparams(sem):
    return pltpu.CompilerParams(dimension_semantics=sem, vmem_limit_bytes=VMEM_LIMIT)


def _split_bf16(x):
    hi = x.astype(BF16)
    lo = (x - hi.astype(F32)).astype(BF16)
    return hi, lo


def _dot(a, b):
    return jnp.dot(a, b, preferred_element_type=F32)


def _dot_nt(a, b):
    return lax.dot_general(a, b, (((1,), (1,)), ((), ())), preferred_element_type=F32)


def _dot_tn(a, b):
    return lax.dot_general(a, b, (((0,), (0,)), ((), ())), preferred_element_type=F32)


def _sigmoid(x):
    return 1.0 / (1.0 + jnp.exp(-x))


def _silu(x):
    return x * _sigmoid(x)


def _gelu_tanh(x):
    c = math.sqrt(2.0 / math.pi)
    return 0.5 * x * (1.0 + jnp.tanh(c * (x + 0.044715 * (x * x * x))))


def _seg_mean_sq(x, avg):
    hi, lo = _split_bf16(x * x)
    return _dot(hi, avg) + _dot(lo, avg)


def _avg_matrix(width, seg):
    idx = np.arange(width) // seg
    return jnp.asarray((idx[:, None] == idx[None, :]).astype(np.float32) / seg, dtype=BF16)


def _inproj_kernel(x_ref, g_ref, w_ref, ogla_ref, os5_ref, onsa_ref):
    x = x_ref[...]
    ms = jnp.mean(x * x, axis=-1, keepdims=True)
    h = (x * lax.rsqrt(ms + RMS_EPS) * g_ref[...]).astype(BF16)
    ogla_ref[...] = _dot(h, w_ref[:, 0:GLA_COLS])
    os5_ref[...] = _dot(h, w_ref[:, GLA_COLS:GLA_COLS + S5_COLS])
    onsa_ref[...] = _dot(h, w_ref[:, GLA_COLS + S5_COLS:])


def _inproj(x2, norm_g, w_all, tm):
    m = x2.shape[0]
    ncols = w_all.shape[1]
    return pl.pallas_call(
        _inproj_kernel,
        grid=(m // tm,),
        in_specs=[pl.BlockSpec((tm, D_MODEL), lambda i: (i, 0)),
                  pl.BlockSpec((1, D_MODEL), lambda i: (0, 0)),
                  pl.BlockSpec((D_MODEL, ncols), lambda i: (0, 0))],
        out_specs=[pl.BlockSpec((tm, GLA_COLS), lambda i: (i, 0)),
                   pl.BlockSpec((tm, S5_COLS), lambda i: (i, 0)),
                   pl.BlockSpec((tm, NSA_COLS), lambda i: (i, 0))],
        out_shape=[jax.ShapeDtypeStruct((m, GLA_COLS), F32),
                   jax.ShapeDtypeStruct((m, S5_COLS), F32),
                   jax.ShapeDtypeStruct((m, NSA_COLS), F32)],
        compiler_params=_cparams(("parallel",)),
        name="inproj",
    )(x2, norm_g.reshape(1, D_MODEL), w_all)


def _inproj_weight(w_in_l):
    sizes = (GLA_QK, GLA_QK, GLA_W, GLA_LOWRANK, GLA_W, S5_W, S5_W,
             NSA_W, NSA_KV_W, NSA_HEADS * N_BRANCH, NSA_W)
    offs = np.concatenate([[0], np.cumsum(sizes)])
    seg = [w_in_l[:, offs[i]:offs[i + 1]] for i in range(len(sizes))]
    gq, gk, gv, glr, gg, su, sg, nq, nkv, ngl, ng = seg

    def pad(a):
        return jnp.pad(a, ((0, 0), (0, LANE - a.shape[1])))

    return jnp.concatenate([gq, gk, gv, gg, pad(glr), su, sg, nq, nkv, ng, pad(ngl)], axis=1).astype(BF16)


GLA_LEVELS = (16, 8, 4, 2, 1)


def _gla_exponent_matrices(tt):
    r = np.arange(tt)[:, None]
    c = np.arange(tt)[None, :]
    same = (r // GLA_CHUNK) == (c // GLA_CHUNK)
    mats = [same & (c <= r), same & (c > r)]
    for s in GLA_LEVELS:
        blk_r = r // s
        odd = (blk_r % 2) == 1
        mats.append(odd & (c >= blk_r * s) & (c <= r))
        mats.append((~odd) & (c > r) & (c <= blk_r * s + s - 1))
    return jnp.asarray(np.stack(mats).astype(np.float32), dtype=BF16)


def _gla_kernel(x_ref, dm_ref, w2_ref, b2_ref, on_ref, avg_ref, y_ref, st_ref, *, tt):
    nchunk = tt // GLA_CHUNK

    @pl.when(pl.program_id(1) == 0)
    def _():
        st_ref[...] = jnp.zeros_like(st_ref)

    x = x_ref[...]
    q = x[:, 0:GLA_QK] * (GLA_DK ** -0.5)
    k = x[:, GLA_QK:2 * GLA_QK]
    v = x[:, 2 * GLA_QK:2 * GLA_QK + GLA_W]
    gate = x[:, 2 * GLA_QK + GLA_W:2 * GLA_QK + 2 * GLA_W]
    lr = x[:, 2 * GLA_QK + 2 * GLA_W:]
    z = _dot(lr.astype(BF16), w2_ref[...]) + b2_ref[...]
    glog = -(jnp.maximum(-z, 0.0) + jnp.log1p(jnp.exp(-jnp.abs(z)))) / GLA_GATE_NORM

    ghi, glo = _split_bf16(glog)
    gcat = jnp.concatenate([ghi, glo], axis=1)

    def expo(i):
        e = _dot(dm_ref[i], gcat)
        return e[:, :GLA_QK] + e[:, GLA_QK:]

    row = lax.broadcasted_iota(jnp.int32, (tt, GLA_QK), 0)
    lane = lax.broadcasted_iota(jnp.int32, (tt, GLA_QK), 1)
    rr = lax.broadcasted_iota(jnp.int32, (GLA_HEADS * tt, tt), 0) % tt
    cc = lax.broadcasted_iota(jnp.int32, (GLA_HEADS * tt, tt), 1)
    head_masks = [(lane // GLA_DK) == h for h in range(GLA_HEADS)]

    def stack_heads(a):
        return jnp.concatenate([jnp.where(m, a, 0.0) for m in head_masks], axis=0).astype(BF16)

    att = jnp.where(rr == cc, _dot_nt(stack_heads(q), k.astype(BF16)), 0.0)
    for li, s in enumerate(GLA_LEVELS):
        odd = ((row // s) % 2) == 1
        qs = jnp.where(odd, q * jnp.exp(expo(2 + 2 * li)), 0.0)
        ks = jnp.where(odd, 0.0, k * jnp.exp(expo(3 + 2 * li)))
        blk = (rr // (2 * s)) == (cc // (2 * s))
        att = att + jnp.where(blk, _dot_nt(stack_heads(qs), ks.astype(BF16)), 0.0)
    att = att.astype(BF16)

    lane_v = lax.broadcasted_iota(jnp.int32, (tt, GLA_W), 1)
    o = jnp.zeros((tt, GLA_W), F32)
    for h in range(GLA_HEADS):
        vh = jnp.where((lane_v // GLA_DV) == h, v, 0.0).astype(BF16)
        o = o + _dot(att[h * tt:(h + 1) * tt], vh)

    bcum = expo(0)
    brev = expo(1)
    qe = (q * jnp.exp(bcum)).astype(BF16)
    ke = (k * jnp.exp(brev)).astype(BF16)
    vb = v.astype(BF16)
    sr = lax.broadcasted_iota(jnp.int32, (GLA_W, GLA_QK), 0)
    sc = lax.broadcasted_iota(jnp.int32, (GLA_W, GLA_QK), 1)
    st_mask = (sr // GLA_DV) == (sc // GLA_DK)
    st = st_ref[...]
    o_inter = []
    for c in range(nchunk):
        lo, hi = c * GLA_CHUNK, (c + 1) * GLA_CHUNK
        o_inter.append(_dot_nt(qe[lo:hi], st.astype(BF16)))
        kv_t = jnp.where(st_mask, _dot_tn(vb[lo:hi], ke[lo:hi]), 0.0)
        st = st * jnp.exp(bcum[hi - 1:hi, :]) + kv_t
    st_ref[...] = st
    o = o + jnp.concatenate(o_inter, axis=0)

    ms = _seg_mean_sq(o, avg_ref[...])
    y_ref[...] = o * lax.rsqrt(ms + RMS_EPS) * on_ref[...] * _silu(gate)


def _gla(ogla, bsz, w2, b2, onorm, tt):
    m = ogla.shape[0]
    nt = m // bsz // tt
    w2p = jnp.pad(w2, ((0, LANE - GLA_LOWRANK), (0, 0))).astype(BF16)
    dmats = _gla_exponent_matrices(tt)
    return pl.pallas_call(
        functools.partial(_gla_kernel, tt=tt),
        grid=(bsz, nt),
        in_specs=[pl.BlockSpec((tt, GLA_COLS), lambda b, t: (b * nt + t, 0)),
                  pl.BlockSpec(dmats.shape, lambda b, t: (0, 0, 0)),
                  pl.BlockSpec((LANE, GLA_QK), lambda b, t: (0, 0)),
                  pl.BlockSpec((1, GLA_QK), lambda b, t: (0, 0)),
                  pl.BlockSpec((1, GLA_W), lambda b, t: (0, 0)),
                  pl.BlockSpec((GLA_W, GLA_W), lambda b, t: (0, 0))],
        out_specs=pl.BlockSpec((tt, GLA_W), lambda b, t: (b * nt + t, 0)),
        out_shape=jax.ShapeDtypeStruct((m, GLA_W), F32),
        scratch_shapes=[pltpu.VMEM((GLA_W, GLA_QK), F32)],
        compiler_params=_cparams(("arbitrary", "arbitrary")),
        name="gla",
    )(ogla, dmats, w2p, b2.reshape(1, GLA_QK), jnp.tile(onorm, GLA_HEADS).reshape(1, GLA_W),
      _avg_matrix(GLA_W, GLA_DV))


def _s5_operators(lam_re, lam_im, log_step, b_re, b_im, c_re, c_im, n_super):
    lr = jnp.minimum(lam_re.astype(F32), -1e-4)
    li = lam_im.astype(F32)
    dt = jnp.exp(log_step.astype(F32))[:, None]

    def apow(tau):
        tau = jnp.asarray(tau, F32)[..., None, None]
        mag = jnp.exp(lr * dt * tau)
        return mag * jnp.cos(li * dt * tau), mag * jnp.sin(li * dt * tau)

    ar, ai = apow(1.0)
    den = lr * lr + li * li
    fr = ((ar - 1.0) * lr + ai * li) / den
    fi = (ai * lr - (ar - 1.0) * li) / den
    br, bi = b_re.astype(F32), b_im.astype(F32)
    bbr = fr[..., None] * br - fi[..., None] * bi
    bbi = fr[..., None] * bi + fi[..., None] * br
    cr, ci = c_re.astype(F32), c_im.astype(F32)
    L = S5_L
    pr, pi = apow(np.arange(L + 1))
    hp = lax.Precision.HIGHEST
    cpr = cr[None] * jnp.swapaxes(pr, 1, 1)[:, :, None, :] - ci[None] * pi[:, :, None, :]
    cpi = cr[None] * pi[:, :, None, :] + ci[None] * pr[:, :, None, :]
    taps = (jnp.einsum('tgcp,gpd->tgcd', cpr[:L], bbr, precision=hp)
            - jnp.einsum('tgcp,gpd->tgcd', cpi[:L], bbi, precision=hp))
    s_idx = np.arange(L)
    lag = s_idx[None, :] - s_idx[:, None]
    kin = jnp.where((lag >= 0)[:, :, None, None, None], taps[np.clip(lag, 0, L - 1)], 0.0)
    kin = jnp.transpose(kin, (2, 0, 4, 1, 3)).reshape(S5_GROUPS, L * S5_CH, L * S5_CH)
    rev = pr[L - 1 - s_idx], pi[L - 1 - s_idx]
    vre = rev[0][..., None] * bbr[None] - rev[1][..., None] * bbi[None]
    vim = rev[0][..., None] * bbi[None] + rev[1][..., None] * bbr[None]
    vre = jnp.transpose(vre, (1, 0, 3, 2)).reshape(S5_GROUPS, L * S5_CH, S5_STATE)
    vim = jnp.transpose(vim, (1, 0, 3, 2)).reshape(S5_GROUPS, L * S5_CH, S5_STATE)
    wre = jnp.transpose(cpr[1:], (1, 3, 0, 2)).reshape(S5_GROUPS, S5_STATE, L * S5_CH)
    wim = -jnp.transpose(cpi[1:], (1, 3, 0, 2)).reshape(S5_GROUPS, S5_STATE, L * S5_CH)

    def pair_diag(a):
        g, kk, nn = a.shape
        a = a.reshape(g // 2, 2, kk, nn)
        zero = jnp.zeros_like(a[:, 0])
        top = jnp.concatenate([a[:, 0], zero], axis=2)
        bot = jnp.concatenate([zero, a[:, 1]], axis=2)
        return jnp.concatenate([top, bot], axis=1)

    nsteps = max(1, int(math.ceil(math.log2(n_super))))
    spr, spi = apow(float(L) * (2.0 ** np.arange(nsteps)))
    spr = spr.reshape(nsteps, 1, S5_GROUPS * S5_STATE)
    spi = spi.reshape(nsteps, 1, S5_GROUPS * S5_STATE)
    return (kin.astype(BF16), pair_diag(vre).astype(BF16), pair_diag(vim).astype(BF16),
            pair_diag(wre).astype(BF16), pair_diag(wim).astype(BF16), spr, spi)


def _s5_kernel(u_ref, kin_ref, vre_ref, vim_ref, wre_ref, wim_ref, spr_ref, spi_ref, y_ref, xr_ref, xi_ref,
               *, n_super, nsteps):
    npair = S5_GROUPS // 2
    for m in range(npair):
        up = jnp.concatenate([u_ref[0, 2 * m], u_ref[0, 2 * m + 1]], axis=1)
        xr_ref[:, m * LANE:(m + 1) * LANE] = _dot(up, vre_ref[m])
        xi_ref[:, m * LANE:(m + 1) * LANE] = _dot(up, vim_ref[m])
    xr = xr_ref[...]
    xi = xi_ref[...]
    row = lax.broadcasted_iota(jnp.int32, xr.shape, 0)
    for kstep in range(nsteps):
        d = 1 << kstep
        sr = jnp.where(row >= d, pltpu.roll(xr, d, 0), 0.0)
        si = jnp.where(row >= d, pltpu.roll(xi, d, 0), 0.0)
        pr = spr_ref[kstep]
        pi = spi_ref[kstep]
        xr, xi = xr + pr * sr - pi * si, xi + pr * si + pi * sr
    er = jnp.where(row >= 1, pltpu.roll(xr, 1, 0), 0.0).astype(BF16)
    ei = jnp.where(row >= 1, pltpu.roll(xi, 1, 0), 0.0).astype(BF16)
    width = S5_L * S5_CH
    for m in range(npair):
        inter = (_dot(er[:, m * LANE:(m + 1) * LANE], wre_ref[m])
                 + _dot(ei[:, m * LANE:(m + 1) * LANE], wim_ref[m]))
        for j in range(2):
            g = 2 * m + j
            y_ref[0, g] = _dot(u_ref[0, g], kin_ref[g]) + inter[:, j * width:(j + 1) * width]


def _s5_conv(u_t, ops):
    bsz, _, n_super, width = u_t.shape
    kin, vre, vim, wre, wim, spr, spi = ops
    nsteps = spr.shape[0]
    nst = S5_GROUPS * S5_STATE

    def full(a):
        nd = a.ndim
        return pl.BlockSpec(a.shape, lambda b: (0,) * nd)

    return pl.pallas_call(
        functools.partial(_s5_kernel, n_super=n_super, nsteps=nsteps),
        grid=(bsz,),
        in_specs=[pl.BlockSpec((1, S5_GROUPS, n_super, width), lambda b: (b, 0, 0, 0)),
                  full(kin), full(vre), full(vim), full(wre), full(wim), full(spr), full(spi)],
        out_specs=pl.BlockSpec((1, S5_GROUPS, n_super, width), lambda b: (b, 0, 0, 0)),
        out_shape=jax.ShapeDtypeStruct((bsz, S5_GROUPS, n_super, width), F32),
        scratch_shapes=[pltpu.VMEM((n_super, nst), F32), pltpu.VMEM((n_super, nst), F32)],
        compiler_params=_cparams(("arbitrary",)),
        name="s5_conv",
    )(u_t, kin, vre, vim, wre, wim, spr, spi)


def _nsa_prep_kernel(x_ref, qg_ref, kg_ref, gb_ref, avgq_ref, avgk_ref,
                     q_ref, kc_ref, vc_ref, ks_ref, vs_ref, kw_ref, vw_ref, gt_ref, sg_ref, *, seq_len):
    x = x_ref[...]
    q = x[:, 0:NSA_W]
    qn = q * lax.rsqrt(_seg_mean_sq(q, avgq_ref[...]) + RMS_EPS) * qg_ref[...] * (NSA_DH ** -0.5)
    for h in range(NSA_HEADS):
        q_ref[h] = qn[:, h * NSA_DH:(h + 1) * NSA_DH].astype(BF16)
    kv = x[:, NSA_W:NSA_W + NSA_KV_W]
    gw = NSA_KV * NSA_DH
    kc_ref[...] = kv[:, 0:gw]
    vc_ref[...] = kv[:, gw:2 * gw]

    def knorm(a, br):
        return a * lax.rsqrt(_seg_mean_sq(a, avgk_ref[...]) + RMS_EPS) * kg_ref[br]

    ksel = knorm(kv[:, 2 * gw:3 * gw], 1)
    vsel = kv[:, 3 * gw:4 * gw]
    kwin = knorm(kv[:, 4 * gw:5 * gw], 2)
    vwin = kv[:, 5 * gw:6 * gw]
    low = lax.broadcasted_iota(jnp.int32, vsel.shape, 1) < NSA_DH

    def with_ones(a, g):
        return jnp.where(low, a if g == 0 else pltpu.roll(a, NSA_DH, 1), 1.0).astype(BF16)

    tm = x.shape[0]
    pos = lax.rem(pl.program_id(0) * tm, seq_len) + lax.broadcasted_iota(jnp.int32, (tm, LANE), 0)
    onehot = jnp.where(lax.broadcasted_iota(jnp.int32, (tm, LANE), 1) == pos // SEL_BLOCK, 1.0, 0.0)
    for g in range(NSA_KV):
        sl = slice(g * NSA_DH, (g + 1) * NSA_DH)
        kpart = jnp.where(low, ksel if g == 0 else pltpu.roll(ksel, NSA_DH, 1), 0.0)
        ks_ref[g] = jnp.concatenate([onehot, kpart], axis=1).astype(BF16)
        vs_ref[g] = with_ones(vsel, g)
        kw_ref[g] = kwin[:, sl].astype(BF16)
        vw_ref[g] = with_ones(vwin, g)
    gate = x[:, NSA_W + NSA_KV_W:2 * NSA_W + NSA_KV_W]
    sgate = _silu(gate)
    for h in range(NSA_HEADS):
        sg_ref[h] = sgate[:, h * NSA_DH:(h + 1) * NSA_DH]
    gl = x[:, 2 * NSA_W + NSA_KV_W:]
    sig = _sigmoid(gl + gb_ref[...])
    per_group = NSA_HPG * N_BRANCH
    gt_ref[0] = sig
    gt_ref[1] = pltpu.roll(sig, LANE - per_group, 1)


def _nsa_prep(onsa, qn_g, kn_g, gate_b, tm, seq_len):
    m = onsa.shape[0]
    gw = NSA_KV * NSA_DH
    assert seq_len % tm == 0 and seq_len // SEL_BLOCK <= LANE
    head = lambda n: pl.BlockSpec((n, tm, NSA_DH), lambda i: (0, i, 0))
    wide = pl.BlockSpec((NSA_KV, tm, LANE), lambda i: (0, i, 0))
    wide2 = pl.BlockSpec((NSA_KV, tm, 2 * LANE), lambda i: (0, i, 0))
    tok = lambda w: pl.BlockSpec((tm, w), lambda i: (i, 0))
    gbp = jnp.pad(gate_b, (0, LANE - gate_b.shape[0])).reshape(1, LANE)
    return pl.pallas_call(
        functools.partial(_nsa_prep_kernel, seq_len=seq_len),
        grid=(m // tm,),
        in_specs=[pl.BlockSpec((tm, NSA_COLS), lambda i: (i, 0)),
                  pl.BlockSpec((1, NSA_W), lambda i: (0, 0)),
                  pl.BlockSpec((N_BRANCH, 1, gw), lambda i: (0, 0, 0)),
                  pl.BlockSpec((1, LANE), lambda i: (0, 0)),
                  pl.BlockSpec((NSA_W, NSA_W), lambda i: (0, 0)),
                  pl.BlockSpec((gw, gw), lambda i: (0, 0))],
        out_specs=[head(NSA_HEADS), tok(gw), tok(gw), wide2, wide, head(NSA_KV), wide,
                   wide, head(NSA_HEADS)],
        out_shape=[jax.ShapeDtypeStruct((NSA_HEADS, m, NSA_DH), BF16),
                   jax.ShapeDtypeStruct((m, gw), F32), jax.ShapeDtypeStruct((m, gw), F32),
                   jax.ShapeDtypeStruct((NSA_KV, m, 2 * LANE), BF16), jax.ShapeDtypeStruct((NSA_KV, m, LANE), BF16),
                   jax.ShapeDtypeStruct((NSA_KV, m, NSA_DH), BF16), jax.ShapeDtypeStruct((NSA_KV, m, LANE), BF16),
                   jax.ShapeDtypeStruct((NSA_KV, m, LANE), F32),
                   jax.ShapeDtypeStruct((NSA_HEADS, m, NSA_DH), F32)],
        compiler_params=_cparams(("parallel",)),
        name="nsa_prep",
    )(onsa, jnp.tile(qn_g, NSA_HEADS).reshape(1, NSA_W),
      jnp.tile(kn_g, (1, NSA_KV)).reshape(N_BRANCH, 1, gw), gbp,
      _avg_matrix(NSA_W, NSA_DH), _avg_matrix(gw, NSA_DH))


def _compress_kernel(a_ref, pt_ref, pb_ref, w1t_ref, w1b_ref, b1_ref, w2_ref, b2_ref, ng_ref, avg_ref, o_ref,
                     *, normalise):
    a = a_ref[0]
    n = a.shape[0]
    h1 = _dot((a + pt_ref[...]).astype(BF16), w1t_ref[...])
    h2 = _dot((a + pb_ref[...]).astype(BF16), w1b_ref[...])
    hid = _gelu_tanh(h1 + pltpu.roll(h2, n - 1, 0) + b1_ref[...])
    out = _dot(hid.astype(BF16), w2_ref[...]) + b2_ref[...]
    if normalise:
        out = out * lax.rsqrt(_seg_mean_sq(out, avg_ref[...]) + RMS_EPS) * ng_ref[...]
    for g in range(NSA_KV):
        o_ref[0, g] = out[:, g * NSA_DH:(g + 1) * NSA_DH].astype(BF16)


def _compress(raw, bsz, pos, w1, b1, w2, b2, norm_g):
    gw = NSA_KV * NSA_DH
    n = raw.shape[0] // bsz // CMP_STRIDE
    a2 = raw.reshape(bsz, n, CMP_STRIDE * gw)
    half = CMP_STRIDE * NSA_DH

    def pos_rows(p):
        return jnp.tile(p[:, None, :], (1, NSA_KV, 1)).reshape(1, CMP_STRIDE * gw)

    def w1_rows(w):
        w = w.reshape(CMP_STRIDE, NSA_DH, CMP_HIDDEN)
        z = jnp.zeros_like(w)
        g0 = jnp.concatenate([w, z], axis=2)
        g1 = jnp.concatenate([z, w], axis=2)
        return jnp.stack([g0, g1], axis=1).reshape(CMP_STRIDE * gw, NSA_KV * CMP_HIDDEN).astype(BF16)

    zero = jnp.zeros_like(w2)
    w2b = jnp.concatenate([jnp.concatenate([w2, zero], axis=1), jnp.concatenate([zero, w2], axis=1)], axis=0)
    normalise = norm_g is not None
    ng = jnp.tile(norm_g if normalise else jnp.ones((NSA_DH,), F32), NSA_KV).reshape(1, gw)
    full = lambda a: pl.BlockSpec(a.shape, lambda b: (0,) * a.ndim)
    args = (pos_rows(pos[:CMP_STRIDE]), pos_rows(pos[CMP_STRIDE:]), w1_rows(w1[:half]), w1_rows(w1[half:]),
            jnp.tile(b1, NSA_KV).reshape(1, -1), w2b.astype(BF16), jnp.tile(b2, NSA_KV).reshape(1, gw), ng,
            _avg_matrix(gw, NSA_DH))
    return pl.pallas_call(
        functools.partial(_compress_kernel, normalise=normalise),
        grid=(bsz,),
        in_specs=[pl.BlockSpec((1, n, CMP_STRIDE * gw), lambda b: (b, 0, 0))] + [full(a) for a in args],
        out_specs=pl.BlockSpec((1, NSA_KV, n, NSA_DH), lambda b: (b, 0, 0, 0)),
        out_shape=jax.ShapeDtypeStruct((bsz, NSA_KV, n, NSA_DH), BF16),
        compiler_params=_cparams(("parallel",)),
        name="nsa_compress",
    )(a2, *args)


def _lane_slabs(a):
    return [a[:, c * LANE:(c + 1) * LANE] for c in range(a.shape[1] // LANE)]


def _row_max(slabs):
    mx = functools.reduce(jnp.maximum, slabs)
    return jnp.broadcast_to(jnp.max(mx, axis=1, keepdims=True), mx.shape)


def _nsa_kernel(q_ref, kc_ref, vc_ref, ovt_ref, ks_ref, vs_ref, kw_ref, vw_ref, gt_ref, sg_ref, o_ref,
                qa_ref, s0_ref, s1_ref, p0_ref, p1_ref, a0_ref, a1_ref, m_ref, acc_ref, y_ref,
                *, tq, tk, wlen, top_k):
    i = pl.program_id(2)
    rows = NSA_HPG * tq
    q4 = q_ref[...].reshape(rows, NSA_DH)
    gates = gt_ref[0]
    heads = [slice(h * tq, (h + 1) * tq) for h in range(NSA_HPG)]

    kc = kc_ref[0, 0]
    n = kc.shape[0]
    s_cmp = _dot_nt(q4, kc)
    start = pl.multiple_of(jnp.maximum(i * tq + tq - wlen, 0), tq)
    s_win = _dot_nt(q4, kw_ref[0, pl.ds(start, wlen), :])

    t_tok = i * tq + lax.broadcasted_iota(jnp.int32, (tq, n), 0)
    c_idx = lax.broadcasted_iota(jnp.int32, (tq, n), 1)
    mask = (c_idx * CMP_STRIDE + (CMP_LEN - 1)) <= t_tok
    real = c_idx < (n - 1)
    psum = jnp.zeros((tq, n), F32)
    for r in heads:
        sm = jnp.where(mask, s_cmp[r], NEG_INF)
        e = jnp.exp(sm - jnp.max(sm, axis=1, keepdims=True))
        denom = jnp.sum(jnp.where(real, e, 0.0), axis=1, keepdims=True)
        p = jnp.where(mask, e, 0.0) / denom
        psum = psum + p
        p0_ref[r, 0:n] = p.astype(BF16)
    o_cmp = _dot(p0_ref[:, 0:n], vc_ref[0, 0])
    imp = _dot_nt(ovt_ref[...], psum.astype(BF16))

    tt = i * tq + lax.broadcasted_iota(jnp.int32, (tq, wlen), 0)
    kpos = start + lax.broadcasted_iota(jnp.int32, (tq, wlen), 1)
    wbias = _lane_slabs(jnp.where((kpos <= tt) & (kpos > tt - WINDOW), 0.0, NEG_INF))
    for r in heads:
        slabs = [a + b for a, b in zip(_lane_slabs(s_win[r]), wbias)]
        m_w = _row_max(slabs)
        for c, sl in enumerate(slabs):
            p1_ref[r, c * LANE:(c + 1) * LANE] = jnp.exp(sl - m_w).astype(BF16)
    acc_w = _dot(p1_ref[:, 0:wlen], vw_ref[0, pl.ds(start, wlen), :])
    o_win = acc_w[:, :NSA_DH] / pltpu.roll(acc_w, NSA_DH, 1)[:, :NSA_DH]

    sid = lax.broadcasted_iota(jnp.int32, imp.shape, 0)
    cur = (i * tq + lax.broadcasted_iota(jnp.int32, imp.shape, 1)) // SEL_BLOCK
    visible = sid <= cur
    forced = (sid == 0) | (sid == cur) | (sid == cur - 1)
    chosen = jnp.where(forced | (visible & (cur < top_k)), 1.0, 0.0)
    val = jnp.where(visible & jnp.logical_not(forced), imp, -jnp.inf)
    sidf = sid.astype(F32)
    for _ in range(max(top_k - 3, 0)):
        mval = jnp.max(val, axis=0, keepdims=True)
        first = jnp.min(jnp.where(val == mval, sidf, float(LANE)), axis=0, keepdims=True)
        pick = sidf == first
        chosen = jnp.where(pick & (mval > -jnp.inf), 1.0, chosen)
        val = jnp.where(pick, -jnp.inf, val)
    selb = jnp.where(jnp.transpose(chosen) > 0.5, 0.0, NEG_INF).astype(BF16)

    for h, r in enumerate(heads):
        c0 = h * N_BRANCH
        y_ref[r] = gates[:, c0:c0 + 1] * o_cmp[r] + gates[:, c0 + 2:c0 + 3] * o_win[r]
        qa_ref[r, 0:LANE] = selb
        qa_ref[r, LANE:LANE + NSA_DH] = q_ref[h]
        qa_ref[r, LANE + NSA_DH:2 * LANE] = jnp.zeros((tq, LANE - NSA_DH), BF16)
    m_ref[...] = jnp.full(m_ref.shape, NEG_INF, F32)
    acc_ref[...] = jnp.zeros(acc_ref.shape, F32)

    def qk(j, s_ref):
        s_ref[...] = _dot_nt(qa_ref[...], ks_ref[0, pl.ds(pl.multiple_of(j * tk, tk), tk), :])

    def soft(s_ref, p_ref, a_ref, bias=None):
        for r in heads:
            slabs = [s_ref[r, c * LANE:(c + 1) * LANE] for c in range(tk // LANE)]
            if bias is not None:
                slabs = [a + b for a, b in zip(slabs, bias)]
            m_prev = m_ref[r]
            m_new = jnp.maximum(m_prev, _row_max(slabs))
            for c, sl in enumerate(slabs):
                p_ref[r, c * LANE:(c + 1) * LANE] = jnp.exp(sl - m_new).astype(BF16)
            a_ref[r] = jnp.exp(m_prev - m_new)
            m_ref[r] = m_new

    def pv(j, p_ref, a_ref):
        v = vs_ref[0, pl.ds(pl.multiple_of(j * tk, tk), tk), :]
        acc_ref[...] = a_ref[...] * acc_ref[...] + _dot(p_ref[:, 0:tk], v)

    n_full = (i * tq) // tk
    qk(0, s0_ref)

    def pair(jj, carry):
        j = 2 * jj
        qk(j + 1, s1_ref)
        soft(s0_ref, p0_ref, a0_ref)
        pv(j, p0_ref, a0_ref)
        qk(j + 2, s0_ref)
        soft(s1_ref, p1_ref, a1_ref)
        pv(j + 1, p1_ref, a1_ref)
        return carry

    lax.fori_loop(0, n_full // 2, pair, 0)

    tt = i * tq + lax.broadcasted_iota(jnp.int32, (tq, tk), 0)
    kpos = n_full * tk + lax.broadcasted_iota(jnp.int32, (tq, tk), 1)
    causal = _lane_slabs(jnp.where(kpos <= tt, 0.0, NEG_INF))
    odd = (n_full % 2) == 1

    @pl.when(odd)
    def _():
        qk(n_full, s1_ref)
        soft(s0_ref, p0_ref, a0_ref)
        pv(n_full - 1, p0_ref, a0_ref)
        soft(s1_ref, p1_ref, a1_ref, causal)
        pv(n_full, p1_ref, a1_ref)

    @pl.when(jnp.logical_not(odd))
    def _():
        soft(s0_ref, p0_ref, a0_ref, causal)
        pv(n_full, p0_ref, a0_ref)

    acc_sel = acc_ref[...]
    o_sel = acc_sel[:, :NSA_DH] / pltpu.roll(acc_sel, NSA_DH, 1)[:, :NSA_DH]

    for h, r in enumerate(heads):
        c1 = h * N_BRANCH + 1
        o_ref[h] = ((y_ref[r] + gates[:, c1:c1 + 1] * o_sel[r]) * sg_ref[h]).astype(BF16)


def _overlap_t(n_cmp_pad, n_sel):
    c = np.arange(n_cmp_pad)
    start, end = c * CMP_STRIDE, c * CMP_STRIDE + CMP_LEN - 1
    s0 = np.arange(LANE) * SEL_BLOCK
    ov = (start[None, :] < s0[:, None] + SEL_BLOCK) & (end[None, :] >= s0[:, None])
    ov = ov & (np.arange(LANE) < n_sel)[:, None]
    return jnp.asarray(ov.astype(np.float32), dtype=BF16)


def _nsa_attn(qh, kc, vc, ks, vs, kw, vw, gates, sg, bsz, tq, tk):
    m = qh.shape[1]
    t = m // bsz
    nq = t // tq
    n = kc.shape[2]
    n_sel = t // SEL_BLOCK
    top_k = min(SEL_TOPK, n_sel)
    wlen = WINDOW + tq
    assert t % tk == 0 and tk % tq == 0 and WINDOW % tq == 0 and t >= wlen and tk % LANE == 0
    assert n_sel <= LANE and n % LANE == 0
    ovt = _overlap_t(n, n_sel)
    rows = NSA_HPG * tq
    pw = max(n, tk, wlen)
    heads = pl.BlockSpec((NSA_HPG, tq, NSA_DH), lambda b, g, i: (g, b * nq + i, 0))
    cmp_spec = pl.BlockSpec((1, 1, n, NSA_DH), lambda b, g, i: (b, g, 0, 0))
    seq = lambda w: pl.BlockSpec((1, t, w), lambda b, g, i: (g, b, 0))
    stat = lambda: pltpu.VMEM((rows, LANE), F32)
    return pl.pallas_call(
        functools.partial(_nsa_kernel, tq=tq, tk=tk, wlen=wlen, top_k=top_k),
        grid=(bsz, NSA_KV, nq),
        in_specs=[heads, cmp_spec, cmp_spec,
                  pl.BlockSpec((LANE, n), lambda b, g, i: (0, 0)),
                  seq(2 * LANE), seq(LANE), seq(NSA_DH), seq(LANE),
                  pl.BlockSpec((1, tq, LANE), lambda b, g, i: (g, b * nq + i, 0)),
                  heads],
        out_specs=heads,
        out_shape=jax.ShapeDtypeStruct((NSA_HEADS, m, NSA_DH), BF16),
        scratch_shapes=[pltpu.VMEM((rows, 2 * LANE), BF16),
                        pltpu.VMEM((rows, tk), F32), pltpu.VMEM((rows, tk), F32),
                        pltpu.VMEM((rows, pw), BF16), pltpu.VMEM((rows, pw), BF16),
                        stat(), stat(), stat(), stat(), pltpu.VMEM((rows, NSA_DH), F32)],
        compiler_params=_cparams(("parallel", "parallel", "arbitrary")),
        name="nsa_attn",
    )(qh, kc, vc, ovt, ks, vs, kw, vw, gates, sg)


def _outproj_kernel(x_ref, ygla_ref, yconv_ref, os5_ref, d_ref, gw_ref, gb_ref, ynsa_ref, wo_ref, o_ref):
    u = os5_ref[:, 0:S5_W]
    sgate = os5_ref[:, S5_W:2 * S5_W]
    y = _gelu_tanh(yconv_ref[...] + d_ref[...] * u)
    z = _dot(y.astype(BF16), gw_ref[...]) + gb_ref[...]
    ys5 = z[:, :S5_W] * _sigmoid(z[:, S5_W:]) * _silu(sgate)
    acc = x_ref[...] + _dot(ygla_ref[...].astype(BF16), wo_ref[0:GLA_W, :])
    acc = acc + _dot(ys5.astype(BF16), wo_ref[GLA_W:GLA_W + S5_W, :])
    base = GLA_W + S5_W
    for h in range(NSA_HEADS):
        acc = acc + _dot(ynsa_ref[h], wo_ref[base + h * NSA_DH:base + (h + 1) * NSA_DH, :])
    o_ref[...] = acc


def _outproj(x2, ygla, yconv, os5, s5_d, glu_w, glu_b, ynsa, w_out, tm):
    m = x2.shape[0]
    tok = lambda w: pl.BlockSpec((tm, w), lambda i: (i, 0))
    head = pl.BlockSpec((NSA_HEADS, tm, NSA_DH), lambda i: (0, i, 0))
    full = lambda a: pl.BlockSpec(a.shape, lambda i: (0,) * a.ndim)
    d = s5_d.reshape(1, S5_W)
    gw = glu_w.astype(BF16)
    gb = glu_b.reshape(1, 2 * S5_W)
    wo = w_out.astype(BF16)
    return pl.pallas_call(
        _outproj_kernel,
        grid=(m // tm,),
        in_specs=[tok(D_MODEL), tok(GLA_W), tok(S5_W), tok(S5_COLS), full(d), full(gw), full(gb),
                  head, full(wo)],
        out_specs=tok(D_MODEL),
        out_shape=jax.ShapeDtypeStruct((m, D_MODEL), F32),
        compiler_params=_cparams(("parallel",)),
        name="outproj",
    )(x2, ygla, yconv, os5, d, gw, gb, ynsa, wo)


def _pick(t, pref):
    while t % pref:
        pref //= 2
    return pref


def _layer(x2, bsz, p):
    m = x2.shape[0]
    t = m // bsz
    tm = _pick(m, 512)
    ogla, os5, onsa = _inproj(x2, p['norm_g'], _inproj_weight(p['w_in']), tm)

    ygla = _gla(ogla, bsz, p['gla_w2'], p['gla_b2'], p['gla_onorm'], _pick(t, 256))

    n_super = t // S5_L
    ops = _s5_operators(p['s5_lam_re'], p['s5_lam_im'], p['s5_log_step'], p['s5_b_re'], p['s5_b_im'],
                        p['s5_c_re'], p['s5_c_im'], n_super)
    u_t = os5[:, :S5_W].astype(BF16).reshape(bsz, n_super, S5_L, S5_GROUPS, S5_CH)
    u_t = jnp.transpose(u_t, (0, 3, 1, 2, 4)).reshape(bsz, S5_GROUPS, n_super, S5_L * S5_CH)
    yconv = _s5_conv(u_t, ops).reshape(bsz, S5_GROUPS, n_super, S5_L, S5_CH)
    yconv = jnp.transpose(yconv, (0, 2, 3, 1, 4)).reshape(m, S5_W)

    qh, kc_raw, vc_raw, ks, vs, kw, vw, gates, sg = _nsa_prep(onsa, p['nsa_qn'], p['nsa_kn'], p['nsa_gate_b'],
                                                              tm, t)
    kc = _compress(kc_raw, bsz, p['nsa_cmp_pos'][0], p['nsa_cmp_w1'][0], p['nsa_cmp_b1'][0],
                   p['nsa_cmp_w2'][0], p['nsa_cmp_b2'][0], p['nsa_kn'][0])
    vc = _compress(vc_raw, bsz, p['nsa_cmp_pos'][1], p['nsa_cmp_w1'][1], p['nsa_cmp_b1'][1],
                   p['nsa_cmp_w2'][1], p['nsa_cmp_b2'][1], None)
    ynsa = _nsa_attn(qh, kc, vc, ks, vs, kw, vw, gates, sg, bsz, 128, 512)

    return _outproj(x2, ygla, yconv, os5, p['s5_d'], p['s5_glu_w'], p['s5_glu_b'], ynsa, p['w_out'], tm)


def kernel(x, norm_g, w_in, gla_w2, gla_b2, gla_onorm, s5_lam_re, s5_lam_im, s5_log_step, s5_b_re, s5_b_im,
           s5_c_re, s5_c_im, s5_d, s5_glu_w, s5_glu_b, nsa_gate_b, nsa_qn, nsa_kn, nsa_cmp_pos, nsa_cmp_w1,
           nsa_cmp_b1, nsa_cmp_w2, nsa_cmp_b2, w_out):
    params = dict(norm_g=norm_g, w_in=w_in, gla_w2=gla_w2, gla_b2=gla_b2, gla_onorm=gla_onorm,
                  s5_lam_re=s5_lam_re, s5_lam_im=s5_lam_im, s5_log_step=s5_log_step, s5_b_re=s5_b_re,
                  s5_b_im=s5_b_im, s5_c_re=s5_c_re, s5_c_im=s5_c_im, s5_d=s5_d, s5_glu_w=s5_glu_w,
                  s5_glu_b=s5_glu_b, nsa_gate_b=nsa_gate_b, nsa_qn=nsa_qn, nsa_kn=nsa_kn,
                  nsa_cmp_pos=nsa_cmp_pos, nsa_cmp_w1=nsa_cmp_w1, nsa_cmp_b1=nsa_cmp_b1,
                  nsa_cmp_w2=nsa_cmp_w2, nsa_cmp_b2=nsa_cmp_b2, w_out=w_out)
    bsz, t, d = x.shape
    x2 = x.reshape(bsz * t, d)
    for layer in range(w_in.shape[0]):
        x2 = _layer(x2, bsz, {k: v[layer] for k, v in params.items()})
    return x2.reshape(bsz, t, d)
```

```python
import functools
import math

import numpy as np
import jax
import jax.numpy as jnp
from jax import lax
from jax.experimental import pallas as pl
from jax.experimental.pallas import tpu as pltpu

F32 = jnp.float32
BF16 = jnp.bfloat16

D_MODEL = 1024
GLA_HEADS, GLA_DK, GLA_DV = 4, 32, 64
GLA_QK = GLA_HEADS * GLA_DK
GLA_W = GLA_HEADS * GLA_DV
GLA_LOWRANK = 16
GLA_GATE_NORM = 16.0
GLA_CHUNK = 32
S5_GROUPS, S5_CH, S5_STATE = 16, 16, 64
S5_W = S5_GROUPS * S5_CH
S5_L = 16
NSA_HEADS, NSA_KV, NSA_DH = 8, 2, 64
NSA_HPG = NSA_HEADS // NSA_KV
NSA_W = NSA_HEADS * NSA_DH
N_BRANCH = 3
CMP_LEN, CMP_STRIDE, CMP_HIDDEN = 32, 16, 256
SEL_BLOCK, SEL_TOPK = 64, 16
WINDOW = 512
RMS_EPS = 1e-6
NEG_INF = -1e30
FORCE_SCORE = 1e4

LANE = 128
VMEM_LIMIT = 56 * 1024 * 1024

GLA_COLS = GLA_QK * 2 + GLA_W * 2 + LANE
S5_COLS = 2 * S5_W
NSA_KV_W = N_BRANCH * 2 * NSA_KV * NSA_DH
NSA_COLS = NSA_W + NSA_KV_W + NSA_W + LANE


def _cparams(sem):
    return pltpu.CompilerParams(dimension_semantics=sem, vmem_limit_bytes=VMEM_LIMIT)


def _split_bf16(x):
    hi = x.astype(BF16)
    lo = (x - hi.astype(F32)).astype(BF16)
    return hi, lo


def _dot(a, b):
    return jnp.dot(a, b, preferred_element_type=F32)


def _dot_nt(a, b):
    return lax.dot_general(a, b, (((1,), (1,)), ((), ())), preferred_element_type=F32)


def _dot_tn(a, b):
    return lax.dot_general(a, b, (((0,), (0,)), ((), ())), preferred_element_type=F32)


def _sigmoid(x):
    return 1.0 / (1.0 + jnp.exp(-x))


def _silu(x):
    return x * _sigmoid(x)


def _gelu_tanh(x):
    c = math.sqrt(2.0 / math.pi)
    return 0.5 * x * (1.0 + jnp.tanh(c * (x + 0.044715 * (x * x * x))))


def _seg_mean_sq(x, avg):
    hi, lo = _split_bf16(x * x)
    return _dot(hi, avg) + _dot(lo, avg)


def _avg_matrix(width, seg):
    idx = np.arange(width) // seg
    return jnp.asarray((idx[:, None] == idx[None, :]).astype(np.float32) / seg, dtype=BF16)


def _inproj_kernel(x_ref, g_ref, w_ref, ogla_ref, os5_ref, onsa_ref):
    x = x_ref[...]
    ms = jnp.mean(x * x, axis=-1, keepdims=True)
    h = (x * lax.rsqrt(ms + RMS_EPS) * g_ref[...]).astype(BF16)
    ogla_ref[...] = _dot(h, w_ref[:, 0:GLA_COLS])
    os5_ref[...] = _dot(h, w_ref[:, GLA_COLS:GLA_COLS + S5_COLS])
    onsa_ref[...] = _dot(h, w_ref[:, GLA_COLS + S5_COLS:])


def _inproj(x2, g_row, w_all, tm):
    m = x2.shape[0]
    ncols = w_all.shape[1]
    return pl.pallas_call(
        _inproj_kernel,
        grid=(m // tm,),
        in_specs=[pl.BlockSpec((tm, D_MODEL), lambda i: (i, 0)),
                  pl.BlockSpec((1, D_MODEL), lambda i: (0, 0)),
                  pl.BlockSpec((D_MODEL, ncols), lambda i: (0, 0))],
        out_specs=[pl.BlockSpec((tm, GLA_COLS), lambda i: (i, 0)),
                   pl.BlockSpec((tm, S5_COLS), lambda i: (i, 0)),
                   pl.BlockSpec((tm, NSA_COLS), lambda i: (i, 0))],
        out_shape=[jax.ShapeDtypeStruct((m, GLA_COLS), F32),
                   jax.ShapeDtypeStruct((m, S5_COLS), F32),
                   jax.ShapeDtypeStruct((m, NSA_COLS), F32)],
        compiler_params=_cparams(("parallel",)),
        name="inproj",
    )(x2, g_row, w_all)


def _inproj_weight(w_in_l):
    sizes = (GLA_QK, GLA_QK, GLA_W, GLA_LOWRANK, GLA_W, S5_W, S5_W,
             NSA_W, NSA_KV_W, NSA_HEADS * N_BRANCH, NSA_W)
    offs = np.concatenate([[0], np.cumsum(sizes)])
    seg = [w_in_l[:, offs[i]:offs[i + 1]] for i in range(len(sizes))]
    gq, gk, gv, glr, gg, su, sg, nq, nkv, ngl, ng = seg

    def pad(a):
        return jnp.pad(a, ((0, 0), (0, LANE - a.shape[1])))

    return jnp.concatenate([gq, gk, gv, gg, pad(glr), su, sg, nq, nkv, ng, pad(ngl)], axis=1).astype(BF16)


GLA_LEVELS = (16, 8, 4, 2, 1)


def _gla_exponent_matrices(tt):
    r = np.arange(tt)[:, None]
    c = np.arange(tt)[None, :]
    same = (r // GLA_CHUNK) == (c // GLA_CHUNK)
    mats = [same & (c <= r), same & (c > r)]
    for s in GLA_LEVELS:
        blk_r = r // s
        odd = (blk_r % 2) == 1
        mats.append(odd & (c >= blk_r * s) & (c <= r))
        mats.append((~odd) & (c > r) & (c <= blk_r * s + s - 1))
    return jnp.asarray(np.stack(mats).astype(np.float32), dtype=BF16)


def _gla_kernel(x_ref, dm_ref, w2_ref, b2_ref, on_ref, avg_ref, y_ref, st_ref, *, tt):
    nchunk = tt // GLA_CHUNK

    @pl.when(pl.program_id(1) == 0)
    def _():
        st_ref[...] = jnp.zeros_like(st_ref)

    x = x_ref[...]
    q = x[:, 0:GLA_QK] * (GLA_DK ** -0.5)
    k = x[:, GLA_QK:2 * GLA_QK]
    v = x[:, 2 * GLA_QK:2 * GLA_QK + GLA_W]
    gate = x[:, 2 * GLA_QK + GLA_W:2 * GLA_QK + 2 * GLA_W]
    lr = x[:, 2 * GLA_QK + 2 * GLA_W:]
    z = _dot(lr.astype(BF16), w2_ref[...]) + b2_ref[...]
    glog = -(jnp.maximum(-z, 0.0) + jnp.log1p(jnp.exp(-jnp.abs(z)))) / GLA_GATE_NORM

    ghi, glo = _split_bf16(glog)
    gcat = jnp.concatenate([ghi, glo], axis=1)

    def expo(i):
        e = _dot(dm_ref[i], gcat)
        return e[:, :GLA_QK] + e[:, GLA_QK:]

    row = lax.broadcasted_iota(jnp.int32, (tt, GLA_QK), 0)
    lane = lax.broadcasted_iota(jnp.int32, (tt, GLA_QK), 1)
    rr = lax.broadcasted_iota(jnp.int32, (GLA_HEADS * tt, tt), 0) % tt
    cc = lax.broadcasted_iota(jnp.int32, (GLA_HEADS * tt, tt), 1)
    head_masks = [(lane // GLA_DK) == h for h in range(GLA_HEADS)]

    def stack_heads(a):
        return jnp.concatenate([jnp.where(m, a, 0.0) for m in head_masks], axis=0).astype(BF16)

    att = jnp.where(rr == cc, _dot_nt(stack_heads(q), k.astype(BF16)), 0.0)
    for li, s in enumerate(GLA_LEVELS):
        odd = ((row // s) % 2) == 1
        qs = jnp.where(odd, q * jnp.exp(expo(2 + 2 * li)), 0.0)
        ks = jnp.where(odd, 0.0, k * jnp.exp(expo(3 + 2 * li)))
        blk = (rr // (2 * s)) == (cc // (2 * s))
        att = att + jnp.where(blk, _dot_nt(stack_heads(qs), ks.astype(BF16)), 0.0)
    att = att.astype(BF16)

    lane_v = lax.broadcasted_iota(jnp.int32, (tt, GLA_W), 1)
    o = jnp.zeros((tt, GLA_W), F32)
    for h in range(GLA_HEADS):
        vh = jnp.where((lane_v // GLA_DV) == h, v, 0.0).astype(BF16)
        o = o + _dot(att[h * tt:(h + 1) * tt], vh)

    bcum = expo(0)
    brev = expo(1)
    qe = (q * jnp.exp(bcum)).astype(BF16)
    ke = (k * jnp.exp(brev)).astype(BF16)
    vb = v.astype(BF16)
    sr = lax.broadcasted_iota(jnp.int32, (GLA_W, GLA_QK), 0)
    sc = lax.broadcasted_iota(jnp.int32, (GLA_W, GLA_QK), 1)
    st_mask = (sr // GLA_DV) == (sc // GLA_DK)
    st = st_ref[...]
    o_inter = []
    for c in range(nchunk):
        lo, hi = c * GLA_CHUNK, (c + 1) * GLA_CHUNK
        o_inter.append(_dot_nt(qe[lo:hi], st.astype(BF16)))
        kv_t = jnp.where(st_mask, _dot_tn(vb[lo:hi], ke[lo:hi]), 0.0)
        st = st * jnp.exp(bcum[hi - 1:hi, :]) + kv_t
    st_ref[...] = st
    o = o + jnp.concatenate(o_inter, axis=0)

    ms = _seg_mean_sq(o, avg_ref[...])
    y_ref[...] = o * lax.rsqrt(ms + RMS_EPS) * on_ref[...] * _silu(gate)


def _gla_params(w2, b2, onorm):
    return (jnp.pad(w2, ((0, LANE - GLA_LOWRANK), (0, 0))).astype(BF16), b2.reshape(1, GLA_QK),
            jnp.tile(onorm, GLA_HEADS).reshape(1, GLA_W))


def _gla(ogla, bsz, gp, tt):
    m = ogla.shape[0]
    nt = m // bsz // tt
    w2p, b2_row, on_row = gp
    dmats = _gla_exponent_matrices(tt)
    return pl.pallas_call(
        functools.partial(_gla_kernel, tt=tt),
        grid=(bsz, nt),
        in_specs=[pl.BlockSpec((tt, GLA_COLS), lambda b, t: (b * nt + t, 0)),
                  pl.BlockSpec(dmats.shape, lambda b, t: (0, 0, 0)),
                  pl.BlockSpec((LANE, GLA_QK), lambda b, t: (0, 0)),
                  pl.BlockSpec((1, GLA_QK), lambda b, t: (0, 0)),
                  pl.BlockSpec((1, GLA_W), lambda b, t: (0, 0)),
                  pl.BlockSpec((GLA_W, GLA_W), lambda b, t: (0, 0))],
        out_specs=pl.BlockSpec((tt, GLA_W), lambda b, t: (b * nt + t, 0)),
        out_shape=jax.ShapeDtypeStruct((m, GLA_W), F32),
        scratch_shapes=[pltpu.VMEM((GLA_W, GLA_QK), F32)],
        compiler_params=_cparams(("arbitrary", "arbitrary")),
        name="gla",
    )(ogla, dmats, w2p, b2_row, on_row, _avg_matrix(GLA_W, GLA_DV))


def _s5_operators(lam_re, lam_im, log_step, b_re, b_im, c_re, c_im, n_super):
    lr = jnp.minimum(lam_re.astype(F32), -1e-4)
    li = lam_im.astype(F32)
    dt = jnp.exp(log_step.astype(F32))[:, None]

    def apow(tau):
        tau = jnp.asarray(tau, F32)[..., None, None]
        mag = jnp.exp(lr * dt * tau)
        return mag * jnp.cos(li * dt * tau), mag * jnp.sin(li * dt * tau)

    ar, ai = apow(1.0)
    den = lr * lr + li * li
    fr = ((ar - 1.0) * lr + ai * li) / den
    fi = (ai * lr - (ar - 1.0) * li) / den
    br, bi = b_re.astype(F32), b_im.astype(F32)
    bbr = fr[..., None] * br - fi[..., None] * bi
    bbi = fr[..., None] * bi + fi[..., None] * br
    cr, ci = c_re.astype(F32), c_im.astype(F32)
    L = S5_L
    pr, pi = apow(np.arange(L + 1))
    hp = lax.Precision.HIGHEST
    cpr = cr[None] * jnp.swapaxes(pr, 1, 1)[:, :, None, :] - ci[None] * pi[:, :, None, :]
    cpi = cr[None] * pi[:, :, None, :] + ci[None] * pr[:, :, None, :]
    taps = (jnp.einsum('tgcp,gpd->tgcd', cpr[:L], bbr, precision=hp)
            - jnp.einsum('tgcp,gpd->tgcd', cpi[:L], bbi, precision=hp))
    s_idx = np.arange(L)
    lag = s_idx[None, :] - s_idx[:, None]
    kin = jnp.where((lag >= 0)[:, :, None, None, None], taps[np.clip(lag, 0, L - 1)], 0.0)
    kin = jnp.transpose(kin, (2, 0, 4, 1, 3)).reshape(S5_GROUPS, L * S5_CH, L * S5_CH)
    rev = pr[L - 1 - s_idx], pi[L - 1 - s_idx]
    vre = rev[0][..., None] * bbr[None] - rev[1][..., None] * bbi[None]
    vim = rev[0][..., None] * bbi[None] + rev[1][..., None] * bbr[None]
    vre = jnp.transpose(vre, (1, 0, 3, 2)).reshape(S5_GROUPS, L * S5_CH, S5_STATE)
    vim = jnp.transpose(vim, (1, 0, 3, 2)).reshape(S5_GROUPS, L * S5_CH, S5_STATE)
    wre = jnp.transpose(cpr[1:], (1, 3, 0, 2)).reshape(S5_GROUPS, S5_STATE, L * S5_CH)
    wim = -jnp.transpose(cpi[1:], (1, 3, 0, 2)).reshape(S5_GROUPS, S5_STATE, L * S5_CH)

    def pair_diag(a):
        g, kk, nn = a.shape
        a = a.reshape(g // 2, 2, kk, nn)
        zero = jnp.zeros_like(a[:, 0])
        top = jnp.concatenate([a[:, 0], zero], axis=2)
        bot = jnp.concatenate([zero, a[:, 1]], axis=2)
        return jnp.concatenate([top, bot], axis=1)

    nsteps = max(1, int(math.ceil(math.log2(n_super))))
    spr, spi = apow(float(L) * (2.0 ** np.arange(nsteps)))
    spr = spr.reshape(nsteps, 1, S5_GROUPS * S5_STATE)
    spi = spi.reshape(nsteps, 1, S5_GROUPS * S5_STATE)
    return (kin.astype(BF16), pair_diag(vre).astype(BF16), pair_diag(vim).astype(BF16),
            pair_diag(wre).astype(BF16), pair_diag(wim).astype(BF16), spr, spi)


def _s5_kernel(u_ref, kin_ref, vre_ref, vim_ref, wre_ref, wim_ref, spr_ref, spi_ref, y_ref, xr_ref, xi_ref,
               *, n_super, nsteps):
    npair = S5_GROUPS // 2
    for m in range(npair):
        up = jnp.concatenate([u_ref[0, 2 * m], u_ref[0, 2 * m + 1]], axis=1)
        xr_ref[:, m * LANE:(m + 1) * LANE] = _dot(up, vre_ref[m])
        xi_ref[:, m * LANE:(m + 1) * LANE] = _dot(up, vim_ref[m])
    xr = xr_ref[...]
    xi = xi_ref[...]
    row = lax.broadcasted_iota(jnp.int32, xr.shape, 0)
    for kstep in range(nsteps):
        d = 1 << kstep
        sr = jnp.where(row >= d, pltpu.roll(xr, d, 0), 0.0)
        si = jnp.where(row >= d, pltpu.roll(xi, d, 0), 0.0)
        pr = spr_ref[kstep]
        pi = spi_ref[kstep]
        xr, xi = xr + pr * sr - pi * si, xi + pr * si + pi * sr
    er = jnp.where(row >= 1, pltpu.roll(xr, 1, 0), 0.0).astype(BF16)
    ei = jnp.where(row >= 1, pltpu.roll(xi, 1, 0), 0.0).astype(BF16)
    width = S5_L * S5_CH
    for m in range(npair):
        inter = (_dot(er[:, m * LANE:(m + 1) * LANE], wre_ref[m])
                 + _dot(ei[:, m * LANE:(m + 1) * LANE], wim_ref[m]))
        for j in range(2):
            g = 2 * m + j
            y_ref[0, g] = _dot(u_ref[0, g], kin_ref[g]) + inter[:, j * width:(j + 1) * width]


def _s5_conv(u_t, ops):
    bsz, _, n_super, width = u_t.shape
    kin, vre, vim, wre, wim, spr, spi = ops
    nsteps = spr.shape[0]
    nst = S5_GROUPS * S5_STATE

    def full(a):
        nd = a.ndim
        return pl.BlockSpec(a.shape, lambda b: (0,) * nd)

    return pl.pallas_call(
        functools.partial(_s5_kernel, n_super=n_super, nsteps=nsteps),
        grid=(bsz,),
        in_specs=[pl.BlockSpec((1, S5_GROUPS, n_super, width), lambda b: (b, 0, 0, 0)),
                  full(kin), full(vre), full(vim), full(wre), full(wim), full(spr), full(spi)],
        out_specs=pl.BlockSpec((1, S5_GROUPS, n_super, width), lambda b: (b, 0, 0, 0)),
        out_shape=jax.ShapeDtypeStruct((bsz, S5_GROUPS, n_super, width), F32),
        scratch_shapes=[pltpu.VMEM((n_super, nst), F32), pltpu.VMEM((n_super, nst), F32)],
        compiler_params=_cparams(("arbitrary",)),
        name="s5_conv",
    )(u_t, kin, vre, vim, wre, wim, spr, spi)


def _nsa_prep_kernel(x_ref, qg_ref, kg_ref, gb_ref, avgq_ref, avgk_ref,
                     q_ref, kc_ref, vc_ref, ks_ref, vs_ref, kw_ref, vw_ref, gt_ref, sg_ref, *, seq_len):
    x = x_ref[...]
    q = x[:, 0:NSA_W]
    qn = q * lax.rsqrt(_seg_mean_sq(q, avgq_ref[...]) + RMS_EPS) * qg_ref[...] * (NSA_DH ** -0.5)
    for h in range(NSA_HEADS):
        q_ref[h] = qn[:, h * NSA_DH:(h + 1) * NSA_DH].astype(BF16)
    kv = x[:, NSA_W:NSA_W + NSA_KV_W]
    gw = NSA_KV * NSA_DH
    kc_ref[...] = kv[:, 0:gw]
    vc_ref[...] = kv[:, gw:2 * gw]

    def knorm(a, br):
        return a * lax.rsqrt(_seg_mean_sq(a, avgk_ref[...]) + RMS_EPS) * kg_ref[br]

    ksel = knorm(kv[:, 2 * gw:3 * gw], 1)
    vsel = kv[:, 3 * gw:4 * gw]
    kwin = knorm(kv[:, 4 * gw:5 * gw], 2)
    vwin = kv[:, 5 * gw:6 * gw]
    low = lax.broadcasted_iota(jnp.int32, vsel.shape, 1) < NSA_DH

    def with_ones(a, g):
        return jnp.where(low, a if g == 0 else pltpu.roll(a, NSA_DH, 1), 1.0).astype(BF16)

    tm = x.shape[0]
    pos = lax.rem(pl.program_id(0) * tm, seq_len) + lax.broadcasted_iota(jnp.int32, (tm, LANE), 0)
    onehot = jnp.where(lax.broadcasted_iota(jnp.int32, (tm, LANE), 1) == pos // SEL_BLOCK, 1.0, 0.0)
    for g in range(NSA_KV):
        sl = slice(g * NSA_DH, (g + 1) * NSA_DH)
        kpart = jnp.where(low, ksel if g == 0 else pltpu.roll(ksel, NSA_DH, 1), 0.0)
        ks_ref[g] = jnp.concatenate([onehot, kpart], axis=1).astype(BF16)
        vs_ref[g] = with_ones(vsel, g)
        kw_ref[g] = kwin[:, sl].astype(BF16)
        vw_ref[g] = with_ones(vwin, g)
    gate = x[:, NSA_W + NSA_KV_W:2 * NSA_W + NSA_KV_W]
    sgate = _silu(gate)
    for h in range(NSA_HEADS):
        sg_ref[h] = sgate[:, h * NSA_DH:(h + 1) * NSA_DH]
    gl = x[:, 2 * NSA_W + NSA_KV_W:]
    sig = _sigmoid(gl + gb_ref[...])
    per_group = NSA_HPG * N_BRANCH
    gt_ref[0] = sig
    gt_ref[1] = pltpu.roll(sig, LANE - per_group, 1)


def _nsa_prep_params(qn_g, kn_g, gate_b):
    gw = NSA_KV * NSA_DH
    return (jnp.tile(qn_g, NSA_HEADS).reshape(1, NSA_W), jnp.tile(kn_g, (1, NSA_KV)).reshape(N_BRANCH, 1, gw),
            jnp.pad(gate_b, (0, LANE - gate_b.shape[0])).reshape(1, LANE))


def _nsa_prep(onsa, pp, tm, seq_len):
    m = onsa.shape[0]
    gw = NSA_KV * NSA_DH
    qg_row, kg_rows, gbp = pp
    assert seq_len % tm == 0 and seq_len // SEL_BLOCK <= LANE
    head = lambda n: pl.BlockSpec((n, tm, NSA_DH), lambda i: (0, i, 0))
    wide = pl.BlockSpec((NSA_KV, tm, LANE), lambda i: (0, i, 0))
    wide2 = pl.BlockSpec((NSA_KV, tm, 2 * LANE), lambda i: (0, i, 0))
    tok = lambda w: pl.BlockSpec((tm, w), lambda i: (i, 0))
    return pl.pallas_call(
        functools.partial(_nsa_prep_kernel, seq_len=seq_len),
        grid=(m // tm,),
        in_specs=[pl.BlockSpec((tm, NSA_COLS), lambda i: (i, 0)),
                  pl.BlockSpec((1, NSA_W), lambda i: (0, 0)),
                  pl.BlockSpec((N_BRANCH, 1, gw), lambda i: (0, 0, 0)),
                  pl.BlockSpec((1, LANE), lambda i: (0, 0)),
                  pl.BlockSpec((NSA_W, NSA_W), lambda i: (0, 0)),
                  pl.BlockSpec((gw, gw), lambda i: (0, 0))],
        out_specs=[head(NSA_HEADS), tok(gw), tok(gw), wide2, wide, head(NSA_KV), wide,
                   wide, head(NSA_HEADS)],
        out_shape=[jax.ShapeDtypeStruct((NSA_HEADS, m, NSA_DH), BF16),
                   jax.ShapeDtypeStruct((m, gw), F32), jax.ShapeDtypeStruct((m, gw), F32),
                   jax.ShapeDtypeStruct((NSA_KV, m, 2 * LANE), BF16), jax.ShapeDtypeStruct((NSA_KV, m, LANE), BF16),
                   jax.ShapeDtypeStruct((NSA_KV, m, NSA_DH), BF16), jax.ShapeDtypeStruct((NSA_KV, m, LANE), BF16),
                   jax.ShapeDtypeStruct((NSA_KV, m, LANE), F32),
                   jax.ShapeDtypeStruct((NSA_HEADS, m, NSA_DH), F32)],
        compiler_params=_cparams(("parallel",)),
        name="nsa_prep",
    )(onsa, qg_row, kg_rows, gbp, _avg_matrix(NSA_W, NSA_DH), _avg_matrix(gw, NSA_DH))


def _compress_kernel(a_ref, pt_ref, pb_ref, w1t_ref, w1b_ref, b1_ref, w2_ref, b2_ref, ng_ref, avg_ref, o_ref,
                     *, normalise):
    a = a_ref[0]
    n = a.shape[0]
    h1 = _dot((a + pt_ref[...]).astype(BF16), w1t_ref[...])
    h2 = _dot((a + pb_ref[...]).astype(BF16), w1b_ref[...])
    hid = _gelu_tanh(h1 + pltpu.roll(h2, n - 1, 0) + b1_ref[...])
    out = _dot(hid.astype(BF16), w2_ref[...]) + b2_ref[...]
    if normalise:
        out = out * lax.rsqrt(_seg_mean_sq(out, avg_ref[...]) + RMS_EPS) * ng_ref[...]
    for g in range(NSA_KV):
        o_ref[0, g] = out[:, g * NSA_DH:(g + 1) * NSA_DH].astype(BF16)


def _compress_params(pos, w1, b1, w2, b2, norm_g):
    gw = NSA_KV * NSA_DH
    half = CMP_STRIDE * NSA_DH

    def pos_rows(p):
        return jnp.tile(p[:, None, :], (1, NSA_KV, 1)).reshape(1, CMP_STRIDE * gw)

    def w1_rows(w):
        w = w.reshape(CMP_STRIDE, NSA_DH, CMP_HIDDEN)
        z = jnp.zeros_like(w)
        g0 = jnp.concatenate([w, z], axis=2)
        g1 = jnp.concatenate([z, w], axis=2)
        return jnp.stack([g0, g1], axis=1).reshape(CMP_STRIDE * gw, NSA_KV * CMP_HIDDEN).astype(BF16)

    zero = jnp.zeros_like(w2)
    w2b = jnp.concatenate([jnp.concatenate([w2, zero], axis=1), jnp.concatenate([zero, w2], axis=1)], axis=0)
    ng = jnp.tile(norm_g, NSA_KV).reshape(1, gw)
    return (pos_rows(pos[:CMP_STRIDE]), pos_rows(pos[CMP_STRIDE:]), w1_rows(w1[:half]), w1_rows(w1[half:]),
            jnp.tile(b1, NSA_KV).reshape(1, -1), w2b.astype(BF16), jnp.tile(b2, NSA_KV).reshape(1, gw), ng)


def _compress(raw, bsz, cp, normalise):
    gw = NSA_KV * NSA_DH
    n = raw.shape[0] // bsz // CMP_STRIDE
    a2 = raw.reshape(bsz, n, CMP_STRIDE * gw)
    full = lambda a: pl.BlockSpec(a.shape, lambda b: (0,) * a.ndim)
    args = tuple(cp) + (_avg_matrix(gw, NSA_DH),)
    return pl.pallas_call(
        functools.partial(_compress_kernel, normalise=normalise),
        grid=(bsz,),
        in_specs=[pl.BlockSpec((1, n, CMP_STRIDE * gw), lambda b: (b, 0, 0))] + [full(a) for a in args],
        out_specs=pl.BlockSpec((1, NSA_KV, n, NSA_DH), lambda b: (b, 0, 0, 0)),
        out_shape=jax.ShapeDtypeStruct((bsz, NSA_KV, n, NSA_DH), BF16),
        compiler_params=_cparams(("parallel",)),
        name="nsa_compress",
    )(a2, *args)


def _lane_slabs(a):
    return [a[:, c * LANE:(c + 1) * LANE] for c in range(a.shape[1] // LANE)]


def _row_max(slabs):
    mx = functools.reduce(jnp.maximum, slabs)
    return jnp.broadcast_to(jnp.max(mx, axis=1, keepdims=True), mx.shape)


def _nsa_kernel(q_ref, kc_ref, vc_ref, ovt_ref, ks_ref, vs_ref, kw_ref, vw_ref, gt_ref, sg_ref, o_ref,
                qa_ref, s0_ref, s1_ref, p0_ref, p1_ref, a0_ref, a1_ref, m_ref, acc_ref, y_ref,
                *, tq, tk, wlen, top_k):
    i = pl.program_id(2)
    rows = NSA_HPG * tq
    q4 = q_ref[...].reshape(rows, NSA_DH)
    gates = gt_ref[0]
    per_head = tq // LANE
    blocks = [(k // per_head, slice(k * LANE, (k + 1) * LANE),
               slice((k % per_head) * LANE, (k % per_head + 1) * LANE)) for k in range(rows // LANE)]

    kc = kc_ref[0, 0]
    n = kc.shape[0]
    s_cmp = _dot_nt(q4, kc)
    start = pl.multiple_of(jnp.maximum(i * tq + tq - wlen, 0), tq)
    s_win = _dot_nt(q4, kw_ref[0, pl.ds(start, wlen), :])

    c_end = lax.broadcasted_iota(jnp.int32, (LANE, n), 1) * CMP_STRIDE + (CMP_LEN - 1)
    psum = [jnp.zeros((LANE, n), F32) for _ in range(per_head)]
    for k, (_, r, tr) in enumerate(blocks):
        mask = c_end <= i * tq + tr.start + lax.broadcasted_iota(jnp.int32, (LANE, n), 0)
        sm = jnp.where(mask, s_cmp[r], NEG_INF)
        mx = jnp.max(sm, axis=1, keepdims=True)
        e = jnp.exp(sm - mx)
        scale = jnp.where(mx > 0.5 * NEG_INF, 1.0 / jnp.sum(e, axis=1, keepdims=True), 0.0)
        p = e * scale
        psum[k % per_head] = psum[k % per_head] + p
        p0_ref[r, 0:n] = p.astype(BF16)
    o_cmp = _dot(p0_ref[:, 0:n], vc_ref[0, 0])
    psum = jnp.concatenate(psum, axis=0) if per_head > 1 else psum[0]
    imp = _dot_nt(ovt_ref[...], psum.astype(BF16))

    tt = i * tq + lax.broadcasted_iota(jnp.int32, (tq, wlen), 0)
    kpos = start + lax.broadcasted_iota(jnp.int32, (tq, wlen), 1)
    wbias = jnp.where((kpos <= tt) & (kpos > tt - WINDOW), 0.0, NEG_INF)
    for _, r, tr in blocks:
        slabs = [a + b for a, b in zip(_lane_slabs(s_win[r]), _lane_slabs(wbias[tr]))]
        m_w = _row_max(slabs)
        for c, sl in enumerate(slabs):
            p1_ref[r, c * LANE:(c + 1) * LANE] = jnp.exp(sl - m_w).astype(BF16)
    acc_w = _dot(p1_ref[:, 0:wlen], vw_ref[0, pl.ds(start, wlen), :])
    o_win = acc_w[:, :NSA_DH] / pltpu.roll(acc_w, NSA_DH, 1)[:, :NSA_DH]

    sid = lax.broadcasted_iota(jnp.int32, imp.shape, 0)
    cur = (i * tq + lax.broadcasted_iota(jnp.int32, imp.shape, 1)) // SEL_BLOCK
    visible = sid <= cur
    forced = (sid == 0) | (sid == cur) | (sid == cur - 1)
    candidate = visible & jnp.logical_not(forced)
    val = jnp.where(candidate, imp, -jnp.inf)
    sidf = sid.astype(F32)
    for _ in range(max(top_k - 3, 0)):
        mval = jnp.max(val, axis=0, keepdims=True)
        first = jnp.min(jnp.where(val == mval, sidf, float(LANE)), axis=0, keepdims=True)
        val = jnp.where(sidf == first, -jnp.inf, val)
    chosen = forced | (visible & (cur < top_k)) | (candidate & (val == -jnp.inf))
    selb = jnp.where(jnp.transpose(jnp.where(chosen, 1.0, 0.0)) > 0.5, 0.0, NEG_INF).astype(BF16)

    for h, r, tr in blocks:
        c0 = h * N_BRANCH
        y_ref[r] = gates[tr, c0:c0 + 1] * o_cmp[r] + gates[tr, c0 + 2:c0 + 3] * o_win[r]
        qa_ref[r, 0:LANE] = selb[tr]
        qa_ref[r, LANE:LANE + NSA_DH] = q_ref[h, tr, :]
        qa_ref[r, LANE + NSA_DH:2 * LANE] = jnp.zeros((LANE, LANE - NSA_DH), BF16)
    m_ref[...] = jnp.full(m_ref.shape, NEG_INF, F32)
    acc_ref[...] = jnp.zeros(acc_ref.shape, F32)

    def qk(j, s_ref):
        s_ref[...] = _dot_nt(qa_ref[...], ks_ref[0, pl.ds(pl.multiple_of(j * tk, tk), tk), :])

    def soft(s_ref, p_ref, a_ref, bias=None):
        for _, r, tr in blocks:
            slabs = [s_ref[r, c * LANE:(c + 1) * LANE] for c in range(tk // LANE)]
            if bias is not None:
                slabs = [a + b for a, b in zip(slabs, _lane_slabs(bias[tr]))]
            m_prev = m_ref[r]
            m_new = jnp.maximum(m_prev, _row_max(slabs))
            for c, sl in enumerate(slabs):
                p_ref[r, c * LANE:(c + 1) * LANE] = jnp.exp(sl - m_new).astype(BF16)
            a_ref[r] = jnp.exp(m_prev - m_new)
            m_ref[r] = m_new

    def pv(j, p_ref, a_ref):
        v = vs_ref[0, pl.ds(pl.multiple_of(j * tk, tk), tk), :]
        acc_ref[...] = a_ref[...] * acc_ref[...] + _dot(p_ref[:, 0:tk], v)

    n_full = (i * tq) // tk
    qk(0, s0_ref)

    def pair(jj, carry):
        j = 2 * jj
        qk(j + 1, s1_ref)
        soft(s0_ref, p0_ref, a0_ref)
        pv(j, p0_ref, a0_ref)
        qk(j + 2, s0_ref)
        soft(s1_ref, p1_ref, a1_ref)
        pv(j + 1, p1_ref, a1_ref)
        return carry

    lax.fori_loop(0, n_full // 2, pair, 0)

    tt = i * tq + lax.broadcasted_iota(jnp.int32, (tq, tk), 0)
    kpos = n_full * tk + lax.broadcasted_iota(jnp.int32, (tq, tk), 1)
    causal = jnp.where(kpos <= tt, 0.0, NEG_INF)
    odd = (n_full % 2) == 1

    @pl.when(odd)
    def _():
        qk(n_full, s1_ref)
        soft(s0_ref, p0_ref, a0_ref)
        pv(n_full - 1, p0_ref, a0_ref)
        soft(s1_ref, p1_ref, a1_ref, causal)
        pv(n_full, p1_ref, a1_ref)

    @pl.when(jnp.logical_not(odd))
    def _():
        soft(s0_ref, p0_ref, a0_ref, causal)
        pv(n_full, p0_ref, a0_ref)

    acc_sel = acc_ref[...]
    o_sel = acc_sel[:, :NSA_DH] / pltpu.roll(acc_sel, NSA_DH, 1)[:, :NSA_DH]

    for h, r, tr in blocks:
        c1 = h * N_BRANCH + 1
        o_ref[h, tr, :] = ((y_ref[r] + gates[tr, c1:c1 + 1] * o_sel[r]) * sg_ref[h, tr, :]).astype(BF16)


def _overlap_t(n_cmp_pad, n_sel):
    c = np.arange(n_cmp_pad)
    start, end = c * CMP_STRIDE, c * CMP_STRIDE + CMP_LEN - 1
    s0 = np.arange(LANE) * SEL_BLOCK
    ov = (start[None, :] < s0[:, None] + SEL_BLOCK) & (end[None, :] >= s0[:, None])
    ov = ov & (np.arange(LANE) < n_sel)[:, None]
    return jnp.asarray(ov.astype(np.float32), dtype=BF16)


def _nsa_attn(qh, kc, vc, ks, vs, kw, vw, gates, sg, bsz, tq, tk):
    m = qh.shape[1]
    t = m // bsz
    nq = t // tq
    n = kc.shape[2]
    n_sel = t // SEL_BLOCK
    top_k = min(SEL_TOPK, n_sel)
    wlen = WINDOW + tq
    assert t % tk == 0 and tk % tq == 0 and WINDOW % tq == 0 and t >= wlen and tk % LANE == 0
    assert n_sel <= LANE and n % LANE == 0 and tq % LANE == 0
    ovt = _overlap_t(n, n_sel)
    rows = NSA_HPG * tq
    pw = max(n, tk, wlen)
    heads = pl.BlockSpec((NSA_HPG, tq, NSA_DH), lambda b, g, i: (g, b * nq + i, 0))
    cmp_spec = pl.BlockSpec((1, 1, n, NSA_DH), lambda b, g, i: (b, g, 0, 0))
    seq = lambda w: pl.BlockSpec((1, t, w), lambda b, g, i: (g, b, 0))
    stat = lambda: pltpu.VMEM((rows, LANE), F32)
    return pl.pallas_call(
        functools.partial(_nsa_kernel, tq=tq, tk=tk, wlen=wlen, top_k=top_k),
        grid=(bsz, NSA_KV, nq),
        in_specs=[heads, cmp_spec, cmp_spec,
                  pl.BlockSpec((LANE, n), lambda b, g, i: (0, 0)),
                  seq(2 * LANE), seq(LANE), seq(NSA_DH), seq(LANE),
                  pl.BlockSpec((1, tq, LANE), lambda b, g, i: (g, b * nq + i, 0)),
                  heads],
        out_specs=heads,
        out_shape=jax.ShapeDtypeStruct((NSA_HEADS, m, NSA_DH), BF16),
        scratch_shapes=[pltpu.VMEM((rows, 2 * LANE), BF16),
                        pltpu.VMEM((rows, tk), F32), pltpu.VMEM((rows, tk), F32),
                        pltpu.VMEM((rows, pw), BF16), pltpu.VMEM((rows, pw), BF16),
                        stat(), stat(), stat(), stat(), pltpu.VMEM((rows, NSA_DH), F32)],
        compiler_params=_cparams(("parallel", "parallel", "arbitrary")),
        name="nsa_attn",
    )(qh, kc, vc, ovt, ks, vs, kw, vw, gates, sg)


def _outproj_kernel(x_ref, ygla_ref, yconv_ref, os5_ref, d_ref, gw_ref, gb_ref, ynsa_ref, wo_ref, o_ref):
    u = os5_ref[:, 0:S5_W]
    sgate = os5_ref[:, S5_W:2 * S5_W]
    y = _gelu_tanh(yconv_ref[...] + d_ref[...] * u)
    z = _dot(y.astype(BF16), gw_ref[...]) + gb_ref[...]
    ys5 = z[:, :S5_W] * _sigmoid(z[:, S5_W:]) * _silu(sgate)
    acc = x_ref[...] + _dot(ygla_ref[...].astype(BF16), wo_ref[0:GLA_W, :])
    acc = acc + _dot(ys5.astype(BF16), wo_ref[GLA_W:GLA_W + S5_W, :])
    base = GLA_W + S5_W
    for h in range(NSA_HEADS):
        acc = acc + _dot(ynsa_ref[h], wo_ref[base + h * NSA_DH:base + (h + 1) * NSA_DH, :])
    o_ref[...] = acc


def _outproj_params(s5_d, glu_w, glu_b, w_out):
    return s5_d.reshape(1, S5_W), glu_w.astype(BF16), glu_b.reshape(1, 2 * S5_W), w_out.astype(BF16)


def _outproj(x2, ygla, yconv, os5, ynsa, op, tm):
    m = x2.shape[0]
    tok = lambda w: pl.BlockSpec((tm, w), lambda i: (i, 0))
    head = pl.BlockSpec((NSA_HEADS, tm, NSA_DH), lambda i: (0, i, 0))
    full = lambda a: pl.BlockSpec(a.shape, lambda i: (0,) * a.ndim)
    d, gw, gb, wo = op
    return pl.pallas_call(
        _outproj_kernel,
        grid=(m // tm,),
        in_specs=[tok(D_MODEL), tok(GLA_W), tok(S5_W), tok(S5_COLS), full(d), full(gw), full(gb),
                  head, full(wo)],
        out_specs=tok(D_MODEL),
        out_shape=jax.ShapeDtypeStruct((m, D_MODEL), F32),
        compiler_params=_cparams(("parallel",)),
        name="outproj",
    )(x2, ygla, yconv, os5, d, gw, gb, ynsa, wo)


def _pick(t, pref):
    while t % pref:
        pref //= 2
    return pref


def _layer_params(p, n_super):
    return dict(
        inproj=(p['norm_g'].reshape(1, D_MODEL), _inproj_weight(p['w_in'])),
        gla=_gla_params(p['gla_w2'], p['gla_b2'], p['gla_onorm']),
        s5=_s5_operators(p['s5_lam_re'], p['s5_lam_im'], p['s5_log_step'], p['s5_b_re'], p['s5_b_im'],
                         p['s5_c_re'], p['s5_c_im'], n_super),
        prep=_nsa_prep_params(p['nsa_qn'], p['nsa_kn'], p['nsa_gate_b']),
        cmp_k=_compress_params(p['nsa_cmp_pos'][0], p['nsa_cmp_w1'][0], p['nsa_cmp_b1'][0],
                               p['nsa_cmp_w2'][0], p['nsa_cmp_b2'][0], p['nsa_kn'][0]),
        cmp_v=_compress_params(p['nsa_cmp_pos'][1], p['nsa_cmp_w1'][1], p['nsa_cmp_b1'][1],
                               p['nsa_cmp_w2'][1], p['nsa_cmp_b2'][1], jnp.ones((NSA_DH,), F32)),
        out=_outproj_params(p['s5_d'], p['s5_glu_w'], p['s5_glu_b'], p['w_out']))


def _layer(x2, bsz, lp):
    m = x2.shape[0]
    t = m // bsz
    tm = _pick(m, 512)
    ogla, os5, onsa = _inproj(x2, *lp['inproj'], tm)

    ygla = _gla(ogla, bsz, lp['gla'], _pick(t, 256))

    n_super = t // S5_L
    u_t = os5[:, :S5_W].astype(BF16).reshape(bsz, n_super, S5_L, S5_GROUPS, S5_CH)
    u_t = jnp.transpose(u_t, (0, 3, 1, 2, 4)).reshape(bsz, S5_GROUPS, n_super, S5_L * S5_CH)
    yconv = _s5_conv(u_t, lp['s5']).reshape(bsz, S5_GROUPS, n_super, S5_L, S5_CH)
    yconv = jnp.transpose(yconv, (0, 2, 3, 1, 4)).reshape(m, S5_W)

    qh, kc_raw, vc_raw, ks, vs, kw, vw, gates, sg = _nsa_prep(onsa, lp['prep'], tm, t)
    kc = _compress(kc_raw, bsz, lp['cmp_k'], True)
    vc = _compress(vc_raw, bsz, lp['cmp_v'], False)
    ynsa = _nsa_attn(qh, kc, vc, ks, vs, kw, vw, gates, sg, bsz, 256, 512)

    return _outproj(x2, ygla, yconv, os5, ynsa, lp['out'], tm)


def kernel(x, norm_g, w_in, gla_w2, gla_b2, gla_onorm, s5_lam_re, s5_lam_im, s5_log_step, s5_b_re, s5_b_im,
           s5_c_re, s5_c_im, s5_d, s5_glu_w, s5_glu_b, nsa_gate_b, nsa_qn, nsa_kn, nsa_cmp_pos, nsa_cmp_w1,
           nsa_cmp_b1, nsa_cmp_w2, nsa_cmp_b2, w_out):
    params = dict(norm_g=norm_g, w_in=w_in, gla_w2=gla_w2, gla_b2=gla_b2, gla_onorm=gla_onorm,
                  s5_lam_re=s5_lam_re, s5_lam_im=s5_lam_im, s5_log_step=s5_log_step, s5_b_re=s5_b_re,
                  s5_b_im=s5_b_im, s5_c_re=s5_c_re, s5_c_im=s5_c_im, s5_d=s5_d, s5_glu_w=s5_glu_w,
                  s5_glu_b=s5_glu_b, nsa_gate_b=nsa_gate_b, nsa_qn=nsa_qn, nsa_kn=nsa_kn,
                  nsa_cmp_pos=nsa_cmp_pos, nsa_cmp_w1=nsa_cmp_w1, nsa_cmp_b1=nsa_cmp_b1,
                  nsa_cmp_w2=nsa_cmp_w2, nsa_cmp_b2=nsa_cmp_b2, w_out=w_out)
    bsz, t, d = x.shape
    x2 = x.reshape(bsz * t, d)
    prepared = jax.vmap(functools.partial(_layer_params, n_super=t // S5_L))(params)
    for layer in range(w_in.shape[0]):
        x2 = _layer(x2, bsz, jax.tree.map(lambda a: a[layer], prepared))
    return x2.reshape(bsz, t, d)
```

```python
import functools
import math

import numpy as np
import jax
import jax.numpy as jnp
from jax import lax
from jax.experimental import pallas as pl
from jax.experimental.pallas import tpu as pltpu

F32 = jnp.float32
BF16 = jnp.bfloat16

D_MODEL = 1024
GLA_HEADS, GLA_DK, GLA_DV = 4, 32, 64
GLA_QK = GLA_HEADS * GLA_DK
GLA_W = GLA_HEADS * GLA_DV
GLA_LOWRANK = 16
GLA_GATE_NORM = 16.0
GLA_CHUNK = 32
S5_GROUPS, S5_CH, S5_STATE = 16, 16, 64
S5_W = S5_GROUPS * S5_CH
S5_L = 16
NSA_HEADS, NSA_KV, NSA_DH = 8, 2, 64
NSA_HPG = NSA_HEADS // NSA_KV
NSA_W = NSA_HEADS * NSA_DH
N_BRANCH = 3
CMP_LEN, CMP_STRIDE, CMP_HIDDEN = 32, 16, 256
SEL_BLOCK, SEL_TOPK = 64, 16
WINDOW = 512
RMS_EPS = 1e-6
NEG_INF = -1e30
FORCE_SCORE = 1e4

LANE = 128
VMEM_LIMIT = 56 * 1024 * 1024

GLA_COLS = GLA_QK * 2 + GLA_W * 2 + LANE
S5_COLS = 2 * S5_W
NSA_KV_W = N_BRANCH * 2 * NSA_KV * NSA_DH
NSA_COLS = NSA_W + NSA_KV_W + NSA_W + LANE


def _cparams(sem):
    return pltpu.CompilerParams(dimension_semantics=sem, vmem_limit_bytes=VMEM_LIMIT)


def _split_bf16(x):
    hi = x.astype(BF16)
    lo = (x - hi.astype(F32)).astype(BF16)
    return hi, lo


def _dot(a, b):
    return jnp.dot(a, b, preferred_element_type=F32)


def _dot_nt(a, b):
    return lax.dot_general(a, b, (((1,), (1,)), ((), ())), preferred_element_type=F32)


def _dot_tn(a, b):
    return lax.dot_general(a, b, (((0,), (0,)), ((), ())), preferred_element_type=F32)


def _sigmoid(x):
    return 1.0 / (1.0 + jnp.exp(-x))


def _silu(x):
    return x * _sigmoid(x)


def _gelu_tanh(x):
    c = math.sqrt(2.0 / math.pi)
    return 0.5 * x * (1.0 + jnp.tanh(c * (x + 0.044715 * (x * x * x))))


def _seg_mean_sq(x, avg):
    hi, lo = _split_bf16(x * x)
    return _dot(hi, avg) + _dot(lo, avg)


def _avg_matrix(width, seg):
    idx = np.arange(width) // seg
    return jnp.asarray((idx[:, None] == idx[None, :]).astype(np.float32) / seg, dtype=BF16)


def _inproj_kernel(x_ref, g_ref, w_ref, qg_ref, kg_ref, gb_ref, avgq_ref, avgk_ref,
                   ogla_ref, os5_ref, u16_ref, *nsa_refs, seq_len):
    x = x_ref[...]
    ms = jnp.mean(x * x, axis=-1, keepdims=True)
    h = (x * lax.rsqrt(ms + RMS_EPS) * g_ref[...]).astype(BF16)
    ogla_ref[...] = _dot(h, w_ref[:, 0:GLA_COLS])
    s5 = _dot(h, w_ref[:, GLA_COLS:GLA_COLS + S5_COLS])
    os5_ref[...] = s5
    u16_ref[...] = s5[:, 0:S5_W].astype(BF16)
    _nsa_prep_body(_dot(h, w_ref[:, GLA_COLS + S5_COLS:]), qg_ref, kg_ref, gb_ref, avgq_ref, avgk_ref,
                   *nsa_refs, seq_len=seq_len)


def _inproj(x2, g_row, w_all, pp, tm, seq_len):
    m = x2.shape[0]
    ncols = w_all.shape[1]
    gw = NSA_KV * NSA_DH
    assert seq_len % tm == 0 and seq_len // SEL_BLOCK <= LANE
    qg_row, kg_rows, gbp = pp
    consts = (g_row, w_all, qg_row, kg_rows, gbp, _avg_matrix(NSA_W, NSA_DH), _avg_matrix(gw, NSA_DH))
    full = lambda a: pl.BlockSpec(a.shape, lambda i: (0,) * a.ndim)
    tok = lambda w: pl.BlockSpec((tm, w), lambda i: (i, 0))
    head = lambda n, w: pl.BlockSpec((n, tm, w), lambda i: (0, i, 0))
    outs = [(tok(GLA_COLS), (m, GLA_COLS), F32), (tok(S5_COLS), (m, S5_COLS), F32), (tok(S5_W), (m, S5_W), BF16),
            (head(NSA_HEADS, NSA_DH), (NSA_HEADS, m, NSA_DH), BF16),
            (tok(gw), (m, gw), F32), (tok(gw), (m, gw), F32),
            (head(NSA_KV, 2 * LANE), (NSA_KV, m, 2 * LANE), BF16), (head(NSA_KV, LANE), (NSA_KV, m, LANE), BF16),
            (head(NSA_KV, NSA_DH), (NSA_KV, m, NSA_DH), BF16), (head(NSA_KV, LANE), (NSA_KV, m, LANE), BF16),
            (head(NSA_KV, LANE), (NSA_KV, m, LANE), F32), (head(NSA_HEADS, NSA_DH), (NSA_HEADS, m, NSA_DH), F32)]
    return pl.pallas_call(
        functools.partial(_inproj_kernel, seq_len=seq_len),
        grid=(m // tm,),
        in_specs=[pl.BlockSpec((tm, D_MODEL), lambda i: (i, 0))] + [full(a) for a in consts],
        out_specs=[o[0] for o in outs],
        out_shape=[jax.ShapeDtypeStruct(o[1], o[2]) for o in outs],
        compiler_params=_cparams(("parallel",)),
        name="inproj",
    )(x2, *consts)


def _inproj_weight(w_in_l):
    sizes = (GLA_QK, GLA_QK, GLA_W, GLA_LOWRANK, GLA_W, S5_W, S5_W,
             NSA_W, NSA_KV_W, NSA_HEADS * N_BRANCH, NSA_W)
    offs = np.concatenate([[0], np.cumsum(sizes)])
    seg = [w_in_l[:, offs[i]:offs[i + 1]] for i in range(len(sizes))]
    gq, gk, gv, glr, gg, su, sg, nq, nkv, ngl, ng = seg

    def pad(a):
        return jnp.pad(a, ((0, 0), (0, LANE - a.shape[1])))

    return jnp.concatenate([gq, gk, gv, gg, pad(glr), su, sg, nq, nkv, ng, pad(ngl)], axis=1).astype(BF16)


GLA_LEVELS = (16, 8, 4, 2, 1)


def _gla_exponent_matrices(tt):
    r = np.arange(tt)[:, None]
    c = np.arange(tt)[None, :]
    same = (r // GLA_CHUNK) == (c // GLA_CHUNK)
    mats = [same & (c <= r), same & (c > r)]
    for s in GLA_LEVELS:
        blk_r = r // s
        odd = (blk_r % 2) == 1
        mats.append(odd & (c >= blk_r * s) & (c <= r))
        mats.append((~odd) & (c > r) & (c <= blk_r * s + s - 1))
    return jnp.asarray(np.stack(mats).astype(np.float32), dtype=BF16)


def _gla_kernel(x_ref, dm_ref, w2_ref, b2_ref, on_ref, avg_ref, y_ref, st_ref, *, tt):
    nchunk = tt // GLA_CHUNK

    @pl.when(pl.program_id(1) == 0)
    def _():
        st_ref[...] = jnp.zeros_like(st_ref)

    x = x_ref[...]
    q = x[:, 0:GLA_QK] * (GLA_DK ** -0.5)
    k = x[:, GLA_QK:2 * GLA_QK]
    v = x[:, 2 * GLA_QK:2 * GLA_QK + GLA_W]
    gate = x[:, 2 * GLA_QK + GLA_W:2 * GLA_QK + 2 * GLA_W]
    lr = x[:, 2 * GLA_QK + 2 * GLA_W:]
    z = _dot(lr.astype(BF16), w2_ref[...]) + b2_ref[...]
    glog = -(jnp.maximum(-z, 0.0) + jnp.log1p(jnp.exp(-jnp.abs(z)))) / GLA_GATE_NORM

    ghi, glo = _split_bf16(glog)
    gcat = jnp.concatenate([ghi, glo], axis=1)

    def expo(i):
        e = _dot(dm_ref[i], gcat)
        return e[:, :GLA_QK] + e[:, GLA_QK:]

    row = lax.broadcasted_iota(jnp.int32, (tt, GLA_QK), 0)
    lane = lax.broadcasted_iota(jnp.int32, (tt, GLA_QK), 1)
    rr = lax.broadcasted_iota(jnp.int32, (GLA_HEADS * tt, tt), 0) % tt
    cc = lax.broadcasted_iota(jnp.int32, (GLA_HEADS * tt, tt), 1)
    head_masks = [(lane // GLA_DK) == h for h in range(GLA_HEADS)]

    def stack_heads(a):
        return jnp.concatenate([jnp.where(m, a, 0.0) for m in head_masks], axis=0).astype(BF16)

    att = jnp.where(rr == cc, _dot_nt(stack_heads(q), k.astype(BF16)), 0.0)
    for li, s in enumerate(GLA_LEVELS):
        odd = ((row // s) % 2) == 1
        qs = jnp.where(odd, q * jnp.exp(expo(2 + 2 * li)), 0.0)
        ks = jnp.where(odd, 0.0, k * jnp.exp(expo(3 + 2 * li)))
        blk = (rr // (2 * s)) == (cc // (2 * s))
        att = att + jnp.where(blk, _dot_nt(stack_heads(qs), ks.astype(BF16)), 0.0)
    att = att.astype(BF16)

    lane_v = lax.broadcasted_iota(jnp.int32, (tt, GLA_W), 1)
    o = jnp.zeros((tt, GLA_W), F32)
    for h in range(GLA_HEADS):
        vh = jnp.where((lane_v // GLA_DV) == h, v, 0.0).astype(BF16)
        o = o + _dot(att[h * tt:(h + 1) * tt], vh)

    bcum = expo(0)
    brev = expo(1)
    qe = (q * jnp.exp(bcum)).astype(BF16)
    ke = (k * jnp.exp(brev)).astype(BF16)
    vb = v.astype(BF16)
    sr = lax.broadcasted_iota(jnp.int32, (GLA_W, GLA_QK), 0)
    sc = lax.broadcasted_iota(jnp.int32, (GLA_W, GLA_QK), 1)
    st_mask = (sr // GLA_DV) == (sc // GLA_DK)
    st = st_ref[...]
    o_inter = []
    for c in range(nchunk):
        lo, hi = c * GLA_CHUNK, (c + 1) * GLA_CHUNK
        o_inter.append(_dot_nt(qe[lo:hi], st.astype(BF16)))
        kv_t = jnp.where(st_mask, _dot_tn(vb[lo:hi], ke[lo:hi]), 0.0)
        st = st * jnp.exp(bcum[hi - 1:hi, :]) + kv_t
    st_ref[...] = st
    o = o + jnp.concatenate(o_inter, axis=0)

    ms = _seg_mean_sq(o, avg_ref[...])
    y_ref[...] = o * lax.rsqrt(ms + RMS_EPS) * on_ref[...] * _silu(gate)


def _gla_params(w2, b2, onorm):
    return (jnp.pad(w2, ((0, LANE - GLA_LOWRANK), (0, 0))).astype(BF16), b2.reshape(1, GLA_QK),
            jnp.tile(onorm, GLA_HEADS).reshape(1, GLA_W))


def _gla(ogla, bsz, gp, tt):
    m = ogla.shape[0]
    nt = m // bsz // tt
    w2p, b2_row, on_row = gp
    dmats = _gla_exponent_matrices(tt)
    return pl.pallas_call(
        functools.partial(_gla_kernel, tt=tt),
        grid=(bsz, nt),
        in_specs=[pl.BlockSpec((tt, GLA_COLS), lambda b, t: (b * nt + t, 0)),
                  pl.BlockSpec(dmats.shape, lambda b, t: (0, 0, 0)),
                  pl.BlockSpec((LANE, GLA_QK), lambda b, t: (0, 0)),
                  pl.BlockSpec((1, GLA_QK), lambda b, t: (0, 0)),
                  pl.BlockSpec((1, GLA_W), lambda b, t: (0, 0)),
                  pl.BlockSpec((GLA_W, GLA_W), lambda b, t: (0, 0))],
        out_specs=pl.BlockSpec((tt, GLA_W), lambda b, t: (b * nt + t, 0)),
        out_shape=jax.ShapeDtypeStruct((m, GLA_W), F32),
        scratch_shapes=[pltpu.VMEM((GLA_W, GLA_QK), F32)],
        compiler_params=_cparams(("arbitrary", "arbitrary")),
        name="gla",
    )(ogla, dmats, w2p, b2_row, on_row, _avg_matrix(GLA_W, GLA_DV))


def _s5_operators(lam_re, lam_im, log_step, b_re, b_im, c_re, c_im, n_super):
    lr = jnp.minimum(lam_re.astype(F32), -1e-4)
    li = lam_im.astype(F32)
    dt = jnp.exp(log_step.astype(F32))[:, None]

    def apow(tau):
        tau = jnp.asarray(tau, F32)[..., None, None]
        mag = jnp.exp(lr * dt * tau)
        return mag * jnp.cos(li * dt * tau), mag * jnp.sin(li * dt * tau)

    ar, ai = apow(1.0)
    den = lr * lr + li * li
    fr = ((ar - 1.0) * lr + ai * li) / den
    fi = (ai * lr - (ar - 1.0) * li) / den
    br, bi = b_re.astype(F32), b_im.astype(F32)
    bbr = fr[..., None] * br - fi[..., None] * bi
    bbi = fr[..., None] * bi + fi[..., None] * br
    cr, ci = c_re.astype(F32), c_im.astype(F32)
    L = S5_L
    pr, pi = apow(np.arange(L + 1))
    hp = lax.Precision.HIGHEST
    cpr = cr * pr[:L, :, None, :] - ci * pi[:L, :, None, :]
    cpi = cr * pi[:L, :, None, :] + ci * pr[:L, :, None, :]
    taps = (jnp.einsum('tgcp,gpd->tgcd', cpr, bbr, precision=hp)
            - jnp.einsum('tgcp,gpd->tgcd', cpi, bbi, precision=hp))
    eye = jnp.eye(S5_GROUPS, dtype=F32)
    nst = S5_GROUPS * S5_STATE
    kbd = jnp.einsum('tgcd,gh->tgdhc', taps, eye).reshape(L, S5_W, S5_W)
    b_in = [jnp.einsum('gpd,gh->gdhp', a, eye).reshape(S5_W, nst) for a in (bbr, bbi)]
    c_out = [jnp.einsum('gcp,gh->gphc', a, eye).reshape(nst, S5_W) for a in (cr, -ci)]
    flat = lambda a: a.reshape(a.shape[0], 1, nst)
    s_idx = np.arange(L)
    nsteps = max(1, int(math.ceil(math.log2(n_super))))
    spr, spi = apow(float(L) * (2.0 ** np.arange(nsteps)))
    return (kbd.astype(BF16), b_in[0].astype(BF16), b_in[1].astype(BF16),
            c_out[0].astype(BF16), c_out[1].astype(BF16),
            flat(pr[L - 1 - s_idx]), flat(pi[L - 1 - s_idx]),
            flat(pr[1:]), flat(pi[1:]),
            flat(spr), flat(spi))


def _s5_kernel(u_ref, kbd_ref, bre_ref, bim_ref, cre_ref, cim_ref, pinr_ref, pini_ref, poutr_ref, pouti_ref,
               spr_ref, spi_ref, y_ref, xr_ref, xi_ref, *, nsteps):
    step = lambda r: slice(r * S5_W, (r + 1) * S5_W)
    for r in range(S5_L):
        zr = _dot(u_ref[0, :, step(r)], bre_ref[...])
        zi = _dot(u_ref[0, :, step(r)], bim_ref[...])
        pr, pi = pinr_ref[r], pini_ref[r]
        if r == 0:
            xr_ref[...] = pr * zr - pi * zi
            xi_ref[...] = pr * zi + pi * zr
        else:
            xr_ref[...] += pr * zr - pi * zi
            xi_ref[...] += pr * zi + pi * zr
    xr = xr_ref[...]
    xi = xi_ref[...]
    row = lax.broadcasted_iota(jnp.int32, xr.shape, 0)
    for kstep in range(nsteps):
        d = 1 << kstep
        sr = jnp.where(row >= d, pltpu.roll(xr, d, 0), 0.0)
        si = jnp.where(row >= d, pltpu.roll(xi, d, 0), 0.0)
        pr, pi = spr_ref[kstep], spi_ref[kstep]
        xr, xi = xr + pr * sr - pi * si, xi + pr * si + pi * sr
    xr_ref[...] = jnp.where(row >= 1, pltpu.roll(xr, 1, 0), 0.0)
    xi_ref[...] = jnp.where(row >= 1, pltpu.roll(xi, 1, 0), 0.0)
    for s in range(S5_L):
        pr, pi = poutr_ref[s], pouti_ref[s]
        er, ei = xr_ref[...], xi_ref[...]
        acc = (_dot((pr * er - pi * ei).astype(BF16), cre_ref[...])
               + _dot((pr * ei + pi * er).astype(BF16), cim_ref[...]))
        for r in range(s + 1):
            acc = acc + _dot(u_ref[0, :, step(r)], kbd_ref[s - r])
        y_ref[0, :, step(s)] = acc


def _s5_conv(u2, ops):
    bsz, n_super, width = u2.shape
    nsteps = ops[-1].shape[0]
    nst = S5_GROUPS * S5_STATE
    full = lambda a: pl.BlockSpec(a.shape, lambda b: (0,) * a.ndim)
    return pl.pallas_call(
        functools.partial(_s5_kernel, nsteps=nsteps),
        grid=(bsz,),
        in_specs=[pl.BlockSpec((1, n_super, width), lambda b: (b, 0, 0))] + [full(a) for a in ops],
        out_specs=pl.BlockSpec((1, n_super, width), lambda b: (b, 0, 0)),
        out_shape=jax.ShapeDtypeStruct((bsz, n_super, width), F32),
        scratch_shapes=[pltpu.VMEM((n_super, nst), F32), pltpu.VMEM((n_super, nst), F32)],
        compiler_params=_cparams(("arbitrary",)),
        name="s5_conv",
    )(u2, *ops)


def _nsa_prep_body(x, qg_ref, kg_ref, gb_ref, avgq_ref, avgk_ref,
                   q_ref, kc_ref, vc_ref, ks_ref, vs_ref, kw_ref, vw_ref, gt_ref, sg_ref, *, seq_len):
    q = x[:, 0:NSA_W]
    qn = q * lax.rsqrt(_seg_mean_sq(q, avgq_ref[...]) + RMS_EPS) * qg_ref[...] * (NSA_DH ** -0.5)
    for h in range(NSA_HEADS):
        q_ref[h] = qn[:, h * NSA_DH:(h + 1) * NSA_DH].astype(BF16)
    kv = x[:, NSA_W:NSA_W + NSA_KV_W]
    gw = NSA_KV * NSA_DH
    kc_ref[...] = kv[:, 0:gw]
    vc_ref[...] = kv[:, gw:2 * gw]

    def knorm(a, br):
        return a * lax.rsqrt(_seg_mean_sq(a, avgk_ref[...]) + RMS_EPS) * kg_ref[br]

    ksel = knorm(kv[:, 2 * gw:3 * gw], 1)
    vsel = kv[:, 3 * gw:4 * gw]
    kwin = knorm(kv[:, 4 * gw:5 * gw], 2)
    vwin = kv[:, 5 * gw:6 * gw]
    low = lax.broadcasted_iota(jnp.int32, vsel.shape, 1) < NSA_DH

    def with_ones(a, g):
        return jnp.where(low, a if g == 0 else pltpu.roll(a, NSA_DH, 1), 1.0).astype(BF16)

    tm = x.shape[0]
    pos = lax.rem(pl.program_id(0) * tm, seq_len) + lax.broadcasted_iota(jnp.int32, (tm, LANE), 0)
    onehot = jnp.where(lax.broadcasted_iota(jnp.int32, (tm, LANE), 1) == pos // SEL_BLOCK, 1.0, 0.0)
    for g in range(NSA_KV):
        sl = slice(g * NSA_DH, (g + 1) * NSA_DH)
        kpart = jnp.where(low, ksel if g == 0 else pltpu.roll(ksel, NSA_DH, 1), 0.0)
        ks_ref[g] = jnp.concatenate([onehot, kpart], axis=1).astype(BF16)
        vs_ref[g] = with_ones(vsel, g)
        kw_ref[g] = kwin[:, sl].astype(BF16)
        vw_ref[g] = with_ones(vwin, g)
    gate = x[:, NSA_W + NSA_KV_W:2 * NSA_W + NSA_KV_W]
    sgate = _silu(gate)
    for h in range(NSA_HEADS):
        sg_ref[h] = sgate[:, h * NSA_DH:(h + 1) * NSA_DH]
    gl = x[:, 2 * NSA_W + NSA_KV_W:]
    sig = _sigmoid(gl + gb_ref[...])
    per_group = NSA_HPG * N_BRANCH
    gt_ref[0] = sig
    gt_ref[1] = pltpu.roll(sig, LANE - per_group, 1)


def _nsa_prep_params(qn_g, kn_g, gate_b):
    gw = NSA_KV * NSA_DH
    return (jnp.tile(qn_g, NSA_HEADS).reshape(1, NSA_W), jnp.tile(kn_g, (1, NSA_KV)).reshape(N_BRANCH, 1, gw),
            jnp.pad(gate_b, (0, LANE - gate_b.shape[0])).reshape(1, LANE))


def _compress_kernel(a_ref, pt_ref, pb_ref, w1t_ref, w1b_ref, b1_ref, w2_ref, b2_ref, ng_ref, avg_ref, o_ref,
                     *, normalise):
    a = a_ref[0]
    n = a.shape[0]
    h1 = _dot((a + pt_ref[...]).astype(BF16), w1t_ref[...])
    h2 = _dot((a + pb_ref[...]).astype(BF16), w1b_ref[...])
    hid = _gelu_tanh(h1 + pltpu.roll(h2, n - 1, 0) + b1_ref[...])
    out = _dot(hid.astype(BF16), w2_ref[...]) + b2_ref[...]
    if normalise:
        out = out * lax.rsqrt(_seg_mean_sq(out, avg_ref[...]) + RMS_EPS) * ng_ref[...]
    for g in range(NSA_KV):
        o_ref[0, g] = out[:, g * NSA_DH:(g + 1) * NSA_DH].astype(BF16)


def _compress_params(pos, w1, b1, w2, b2, norm_g):
    gw = NSA_KV * NSA_DH
    half = CMP_STRIDE * NSA_DH

    def pos_rows(p):
        return jnp.tile(p[:, None, :], (1, NSA_KV, 1)).reshape(1, CMP_STRIDE * gw)

    def w1_rows(w):
        w = w.reshape(CMP_STRIDE, NSA_DH, CMP_HIDDEN)
        z = jnp.zeros_like(w)
        g0 = jnp.concatenate([w, z], axis=2)
        g1 = jnp.concatenate([z, w], axis=2)
        return jnp.stack([g0, g1], axis=1).reshape(CMP_STRIDE * gw, NSA_KV * CMP_HIDDEN).astype(BF16)

    zero = jnp.zeros_like(w2)
    w2b = jnp.concatenate([jnp.concatenate([w2, zero], axis=1), jnp.concatenate([zero, w2], axis=1)], axis=0)
    ng = jnp.tile(norm_g, NSA_KV).reshape(1, gw)
    return (pos_rows(pos[:CMP_STRIDE]), pos_rows(pos[CMP_STRIDE:]), w1_rows(w1[:half]), w1_rows(w1[half:]),
            jnp.tile(b1, NSA_KV).reshape(1, -1), w2b.astype(BF16), jnp.tile(b2, NSA_KV).reshape(1, gw), ng)


def _compress(raw, bsz, cp, normalise):
    gw = NSA_KV * NSA_DH
    n = raw.shape[0] // bsz // CMP_STRIDE
    a2 = raw.reshape(bsz, n, CMP_STRIDE * gw)
    full = lambda a: pl.BlockSpec(a.shape, lambda b: (0,) * a.ndim)
    args = tuple(cp) + (_avg_matrix(gw, NSA_DH),)
    return pl.pallas_call(
        functools.partial(_compress_kernel, normalise=normalise),
        grid=(bsz,),
        in_specs=[pl.BlockSpec((1, n, CMP_STRIDE * gw), lambda b: (b, 0, 0))] + [full(a) for a in args],
        out_specs=pl.BlockSpec((1, NSA_KV, n, NSA_DH), lambda b: (b, 0, 0, 0)),
        out_shape=jax.ShapeDtypeStruct((bsz, NSA_KV, n, NSA_DH), BF16),
        compiler_params=_cparams(("parallel",)),
        name="nsa_compress",
    )(a2, *args)


def _lane_slabs(a):
    return [a[:, c * LANE:(c + 1) * LANE] for c in range(a.shape[1] // LANE)]


def _row_max(slabs):
    mx = functools.reduce(jnp.maximum, slabs)
    return jnp.broadcast_to(jnp.max(mx, axis=1, keepdims=True), mx.shape)


def _nsa_kernel(q_ref, kc_ref, vc_ref, ovt_ref, ks_ref, vs_ref, kw_ref, vw_ref, gt_ref, sg_ref, o_ref,
                qa_ref, s0_ref, s1_ref, p0_ref, p1_ref, a0_ref, a1_ref, m_ref, acc_ref, y_ref,
                *, tq, tk, wlen, top_k):
    i = pl.program_id(2)
    rows = NSA_HPG * tq
    q4 = q_ref[...].reshape(rows, NSA_DH)
    gates = gt_ref[0]
    per_head = tq // LANE
    blocks = [(k // per_head, slice(k * LANE, (k + 1) * LANE),
               slice((k % per_head) * LANE, (k % per_head + 1) * LANE)) for k in range(rows // LANE)]

    kc = kc_ref[0, 0]
    n = kc.shape[0]
    s_cmp = _dot_nt(q4, kc)
    start = pl.multiple_of(jnp.maximum(i * tq + tq - wlen, 0), tq)
    s_win = _dot_nt(q4, kw_ref[0, pl.ds(start, wlen), :])

    c_end = lax.broadcasted_iota(jnp.int32, (LANE, n), 1) * CMP_STRIDE + (CMP_LEN - 1)
    psum = [jnp.zeros((LANE, n), F32) for _ in range(per_head)]
    for k, (_, r, tr) in enumerate(blocks):
        mask = c_end <= i * tq + tr.start + lax.broadcasted_iota(jnp.int32, (LANE, n), 0)
        sm = jnp.where(mask, s_cmp[r], NEG_INF)
        mx = jnp.max(sm, axis=1, keepdims=True)
        e = jnp.exp(sm - mx)
        scale = jnp.where(mx > 0.5 * NEG_INF, 1.0 / jnp.sum(e, axis=1, keepdims=True), 0.0)
        p = e * scale
        psum[k % per_head] = psum[k % per_head] + p
        p0_ref[r, 0:n] = p.astype(BF16)
    o_cmp = _dot(p0_ref[:, 0:n], vc_ref[0, 0])
    psum = jnp.concatenate(psum, axis=0) if per_head > 1 else psum[0]
    imp = _dot_nt(ovt_ref[...], psum.astype(BF16))

    tt = i * tq + lax.broadcasted_iota(jnp.int32, (tq, wlen), 0)
    kpos = start + lax.broadcasted_iota(jnp.int32, (tq, wlen), 1)
    wbias = jnp.where((kpos <= tt) & (kpos > tt - WINDOW), 0.0, NEG_INF)
    for _, r, tr in blocks:
        slabs = [a + b for a, b in zip(_lane_slabs(s_win[r]), _lane_slabs(wbias[tr]))]
        m_w = _row_max(slabs)
        for c, sl in enumerate(slabs):
            p1_ref[r, c * LANE:(c + 1) * LANE] = jnp.exp(sl - m_w).astype(BF16)
    acc_w = _dot(p1_ref[:, 0:wlen], vw_ref[0, pl.ds(start, wlen), :])
    o_win = acc_w[:, :NSA_DH] / pltpu.roll(acc_w, NSA_DH, 1)[:, :NSA_DH]

    sid = lax.broadcasted_iota(jnp.int32, imp.shape, 0)
    cur = (i * tq + lax.broadcasted_iota(jnp.int32, imp.shape, 1)) // SEL_BLOCK
    visible = sid <= cur
    forced = (sid == 0) | (sid == cur) | (sid == cur - 1)
    candidate = visible & jnp.logical_not(forced)
    val = jnp.where(candidate, imp, -jnp.inf)
    sidf = sid.astype(F32)
    for _ in range(max(top_k - 3, 0)):
        mval = jnp.max(val, axis=0, keepdims=True)
        first = jnp.min(jnp.where(val == mval, sidf, float(LANE)), axis=0, keepdims=True)
        val = jnp.where(sidf == first, -jnp.inf, val)
    chosen = forced | (visible & (cur < top_k)) | (candidate & (val == -jnp.inf))
    selb = jnp.where(jnp.transpose(jnp.where(chosen, 1.0, 0.0)) > 0.5, 0.0, NEG_INF).astype(BF16)

    for h, r, tr in blocks:
        c0 = h * N_BRANCH
        y_ref[r] = gates[tr, c0:c0 + 1] * o_cmp[r] + gates[tr, c0 + 2:c0 + 3] * o_win[r]
        qa_ref[r, 0:LANE] = selb[tr]
        qa_ref[r, LANE:LANE + NSA_DH] = q_ref[h, tr, :]
        qa_ref[r, LANE + NSA_DH:2 * LANE] = jnp.zeros((LANE, LANE - NSA_DH), BF16)
    m_ref[...] = jnp.full(m_ref.shape, NEG_INF, F32)
    acc_ref[...] = jnp.zeros(acc_ref.shape, F32)

    def qk(j, s_ref):
        s_ref[...] = _dot_nt(qa_ref[...], ks_ref[0, pl.ds(pl.multiple_of(j * tk, tk), tk), :])

    def soft(s_ref, p_ref, a_ref, bias=None):
        for _, r, tr in blocks:
            slabs = [s_ref[r, c * LANE:(c + 1) * LANE] for c in range(tk // LANE)]
            if bias is not None:
                slabs = [a + b for a, b in zip(slabs, _lane_slabs(bias[tr]))]
            m_prev = m_ref[r]
            m_new = jnp.maximum(m_prev, _row_max(slabs))
            for c, sl in enumerate(slabs):
                p_ref[r, c * LANE:(c + 1) * LANE] = jnp.exp(sl - m_new).astype(BF16)
            a_ref[r] = jnp.exp(m_prev - m_new)
            m_ref[r] = m_new

    def pv(j, p_ref, a_ref):
        v = vs_ref[0, pl.ds(pl.multiple_of(j * tk, tk), tk), :]
        acc_ref[...] = a_ref[...] * acc_ref[...] + _dot(p_ref[:, 0:tk], v)

    n_full = (i * tq) // tk
    qk(0, s0_ref)

    def pair(jj, carry):
        j = 2 * jj
        qk(j + 1, s1_ref)
        soft(s0_ref, p0_ref, a0_ref)
        pv(j, p0_ref, a0_ref)
        qk(j + 2, s0_ref)
        soft(s1_ref, p1_ref, a1_ref)
        pv(j + 1, p1_ref, a1_ref)
        return carry

    lax.fori_loop(0, n_full // 2, pair, 0)

    tt = i * tq + lax.broadcasted_iota(jnp.int32, (tq, tk), 0)
    kpos = n_full * tk + lax.broadcasted_iota(jnp.int32, (tq, tk), 1)
    causal = jnp.where(kpos <= tt, 0.0, NEG_INF)
    odd = (n_full % 2) == 1

    @pl.when(odd)
    def _():
        qk(n_full, s1_ref)
        soft(s0_ref, p0_ref, a0_ref)
        pv(n_full - 1, p0_ref, a0_ref)
        soft(s1_ref, p1_ref, a1_ref, causal)
        pv(n_full, p1_ref, a1_ref)

    @pl.when(jnp.logical_not(odd))
    def _():
        soft(s0_ref, p0_ref, a0_ref, causal)
        pv(n_full, p0_ref, a0_ref)

    acc_sel = acc_ref[...]
    o_sel = acc_sel[:, :NSA_DH] / pltpu.roll(acc_sel, NSA_DH, 1)[:, :NSA_DH]

    for h, r, tr in blocks:
        c1 = h * N_BRANCH + 1
        o_ref[h, tr, :] = ((y_ref[r] + gates[tr, c1:c1 + 1] * o_sel[r]) * sg_ref[h, tr, :]).astype(BF16)


def _overlap_t(n_cmp_pad, n_sel):
    c = np.arange(n_cmp_pad)
    start, end = c * CMP_STRIDE, c * CMP_STRIDE + CMP_LEN - 1
    s0 = np.arange(LANE) * SEL_BLOCK
    ov = (start[None, :] < s0[:, None] + SEL_BLOCK) & (end[None, :] >= s0[:, None])
    ov = ov & (np.arange(LANE) < n_sel)[:, None]
    return jnp.asarray(ov.astype(np.float32), dtype=BF16)


def _nsa_attn(qh, kc, vc, ks, vs, kw, vw, gates, sg, bsz, tq, tk):
    m = qh.shape[1]
    t = m // bsz
    nq = t // tq
    n = kc.shape[2]
    n_sel = t // SEL_BLOCK
    top_k = min(SEL_TOPK, n_sel)
    wlen = WINDOW + tq
    assert t % tk == 0 and tk % tq == 0 and WINDOW % tq == 0 and t >= wlen and tk % LANE == 0
    assert n_sel <= LANE and n % LANE == 0 and tq % LANE == 0
    ovt = _overlap_t(n, n_sel)
    rows = NSA_HPG * tq
    pw = max(n, tk, wlen)
    heads = pl.BlockSpec((NSA_HPG, tq, NSA_DH), lambda b, g, i: (g, b * nq + i, 0))
    cmp_spec = pl.BlockSpec((1, 1, n, NSA_DH), lambda b, g, i: (b, g, 0, 0))
    seq = lambda w: pl.BlockSpec((1, t, w), lambda b, g, i: (g, b, 0))
    stat = lambda: pltpu.VMEM((rows, LANE), F32)
    return pl.pallas_call(
        functools.partial(_nsa_kernel, tq=tq, tk=tk, wlen=wlen, top_k=top_k),
        grid=(bsz, NSA_KV, nq),
        in_specs=[heads, cmp_spec, cmp_spec,
                  pl.BlockSpec((LANE, n), lambda b, g, i: (0, 0)),
                  seq(2 * LANE), seq(LANE), seq(NSA_DH), seq(LANE),
                  pl.BlockSpec((1, tq, LANE), lambda b, g, i: (g, b * nq + i, 0)),
                  heads],
        out_specs=heads,
        out_shape=jax.ShapeDtypeStruct((NSA_HEADS, m, NSA_DH), BF16),
        scratch_shapes=[pltpu.VMEM((rows, 2 * LANE), BF16),
                        pltpu.VMEM((rows, tk), F32), pltpu.VMEM((rows, tk), F32),
                        pltpu.VMEM((rows, pw), BF16), pltpu.VMEM((rows, pw), BF16),
                        stat(), stat(), stat(), stat(), pltpu.VMEM((rows, NSA_DH), F32)],
        compiler_params=_cparams(("parallel", "parallel", "arbitrary")),
        name="nsa_attn",
    )(qh, kc, vc, ovt, ks, vs, kw, vw, gates, sg)


def _outproj_kernel(x_ref, ygla_ref, yconv_ref, os5_ref, d_ref, gw_ref, gb_ref, ynsa_ref, wo_ref, o_ref):
    u = os5_ref[:, 0:S5_W]
    sgate = os5_ref[:, S5_W:2 * S5_W]
    y = _gelu_tanh(yconv_ref[...] + d_ref[...] * u)
    z = _dot(y.astype(BF16), gw_ref[...]) + gb_ref[...]
    ys5 = z[:, :S5_W] * _sigmoid(z[:, S5_W:]) * _silu(sgate)
    acc = x_ref[...] + _dot(ygla_ref[...].astype(BF16), wo_ref[0:GLA_W, :])
    acc = acc + _dot(ys5.astype(BF16), wo_ref[GLA_W:GLA_W + S5_W, :])
    base = GLA_W + S5_W
    for h in range(NSA_HEADS):
        acc = acc + _dot(ynsa_ref[h], wo_ref[base + h * NSA_DH:base + (h + 1) * NSA_DH, :])
    o_ref[...] = acc


def _outproj_params(s5_d, glu_w, glu_b, w_out):
    return s5_d.reshape(1, S5_W), glu_w.astype(BF16), glu_b.reshape(1, 2 * S5_W), w_out.astype(BF16)


def _outproj(x2, ygla, yconv, os5, ynsa, op, tm):
    m = x2.shape[0]
    tok = lambda w: pl.BlockSpec((tm, w), lambda i: (i, 0))
    head = pl.BlockSpec((NSA_HEADS, tm, NSA_DH), lambda i: (0, i, 0))
    full = lambda a: pl.BlockSpec(a.shape, lambda i: (0,) * a.ndim)
    d, gw, gb, wo = op
    return pl.pallas_call(
        _outproj_kernel,
        grid=(m // tm,),
        in_specs=[tok(D_MODEL), tok(GLA_W), tok(S5_W), tok(S5_COLS), full(d), full(gw), full(gb),
                  head, full(wo)],
        out_specs=tok(D_MODEL),
        out_shape=jax.ShapeDtypeStruct((m, D_MODEL), F32),
        compiler_params=_cparams(("parallel",)),
        name="outproj",
    )(x2, ygla, yconv, os5, d, gw, gb, ynsa, wo)


def _pick(t, pref):
    while t % pref:
        pref //= 2
    return pref


def _layer_params(p, n_super):
    return dict(
        inproj=(p['norm_g'].reshape(1, D_MODEL), _inproj_weight(p['w_in'])),
        gla=_gla_params(p['gla_w2'], p['gla_b2'], p['gla_onorm']),
        s5=_s5_operators(p['s5_lam_re'], p['s5_lam_im'], p['s5_log_step'], p['s5_b_re'], p['s5_b_im'],
                         p['s5_c_re'], p['s5_c_im'], n_super),
        prep=_nsa_prep_params(p['nsa_qn'], p['nsa_kn'], p['nsa_gate_b']),
        cmp_k=_compress_params(p['nsa_cmp_pos'][0], p['nsa_cmp_w1'][0], p['nsa_cmp_b1'][0],
                               p['nsa_cmp_w2'][0], p['nsa_cmp_b2'][0], p['nsa_kn'][0]),
        cmp_v=_compress_params(p['nsa_cmp_pos'][1], p['nsa_cmp_w1'][1], p['nsa_cmp_b1'][1],
                               p['nsa_cmp_w2'][1], p['nsa_cmp_b2'][1], jnp.ones((NSA_DH,), F32)),
        out=_outproj_params(p['s5_d'], p['s5_glu_w'], p['s5_glu_b'], p['w_out']))


def _layer(x2, bsz, lp):
    m = x2.shape[0]
    t = m // bsz
    tm = _pick(m, 512)
    ogla, os5, u16, qh, kc_raw, vc_raw, ks, vs, kw, vw, gates, sg = _inproj(x2, *lp['inproj'], lp['prep'], tm, t)

    ygla = _gla(ogla, bsz, lp['gla'], _pick(t, 256))

    yconv = _s5_conv(u16.reshape(bsz, t // S5_L, S5_L * S5_W), lp['s5']).reshape(m, S5_W)

    kc = _compress(kc_raw, bsz, lp['cmp_k'], True)
    vc = _compress(vc_raw, bsz, lp['cmp_v'], False)
    ynsa = _nsa_attn(qh, kc, vc, ks, vs, kw, vw, gates, sg, bsz, 256, 512)

    return _outproj(x2, ygla, yconv, os5, ynsa, lp['out'], tm)


def kernel(x, norm_g, w_in, gla_w2, gla_b2, gla_onorm, s5_lam_re, s5_lam_im, s5_log_step, s5_b_re, s5_b_im,
           s5_c_re, s5_c_im, s5_d, s5_glu_w, s5_glu_b, nsa_gate_b, nsa_qn, nsa_kn, nsa_cmp_pos, nsa_cmp_w1,
           nsa_cmp_b1, nsa_cmp_w2, nsa_cmp_b2, w_out):
    params = dict(norm_g=norm_g, w_in=w_in, gla_w2=gla_w2, gla_b2=gla_b2, gla_onorm=gla_onorm,
                  s5_lam_re=s5_lam_re, s5_lam_im=s5_lam_im, s5_log_step=s5_log_step, s5_b_re=s5_b_re,
                  s5_b_im=s5_b_im, s5_c_re=s5_c_re, s5_c_im=s5_c_im, s5_d=s5_d, s5_glu_w=s5_glu_w,
                  s5_glu_b=s5_glu_b, nsa_gate_b=nsa_gate_b, nsa_qn=nsa_qn, nsa_kn=nsa_kn,
                  nsa_cmp_pos=nsa_cmp_pos, nsa_cmp_w1=nsa_cmp_w1, nsa_cmp_b1=nsa_cmp_b1,
                  nsa_cmp_w2=nsa_cmp_w2, nsa_cmp_b2=nsa_cmp_b2, w_out=w_out)
    bsz, t, d = x.shape
    x2 = x.reshape(bsz * t, d)
    prepared = jax.vmap(functools.partial(_layer_params, n_super=t // S5_L))(params)
    for layer in range(w_in.shape[0]):
        x2 = _layer(x2, bsz, jax.tree.map(lambda a: a[layer], prepared))
    return x2.reshape(bsz, t, d)
```

```python
import functools
import math

import numpy as np
import jax
import jax.numpy as jnp
from jax import lax
from jax.experimental import pallas as pl
from jax.experimental.pallas import tpu as pltpu

F32 = jnp.float32
BF16 = jnp.bfloat16

D_MODEL = 1024
GLA_HEADS, GLA_DK, GLA_DV = 4, 32, 64
GLA_QK = GLA_HEADS * GLA_DK
GLA_W = GLA_HEADS * GLA_DV
GLA_LOWRANK = 16
GLA_GATE_NORM = 16.0
GLA_CHUNK = 32
S5_GROUPS, S5_CH, S5_STATE = 16, 16, 64
S5_W = S5_GROUPS * S5_CH
S5_L = 16
NSA_HEADS, NSA_KV, NSA_DH = 8, 2, 64
NSA_HPG = NSA_HEADS // NSA_KV
NSA_W = NSA_HEADS * NSA_DH
N_BRANCH = 3
CMP_LEN, CMP_STRIDE, CMP_HIDDEN = 32, 16, 256
SEL_BLOCK, SEL_TOPK = 64, 16
WINDOW = 512
RMS_EPS = 1e-6
NEG_INF = -1e30
FORCE_SCORE = 1e4

LANE = 128
VMEM_LIMIT = 56 * 1024 * 1024

GLA_COLS = GLA_QK * 2 + GLA_W * 2 + LANE
S5_COLS = 2 * S5_W
NSA_KV_W = N_BRANCH * 2 * NSA_KV * NSA_DH
NSA_COLS = NSA_W + NSA_KV_W + NSA_W + LANE


def _cparams(sem):
    return pltpu.CompilerParams(dimension_semantics=sem, vmem_limit_bytes=VMEM_LIMIT)


def _split_bf16(x):
    hi = x.astype(BF16)
    lo = (x - hi.astype(F32)).astype(BF16)
    return hi, lo


def _dot(a, b):
    return jnp.dot(a, b, preferred_element_type=F32)


def _dot_nt(a, b):
    return lax.dot_general(a, b, (((1,), (1,)), ((), ())), preferred_element_type=F32)


def _dot_tn(a, b):
    return lax.dot_general(a, b, (((0,), (0,)), ((), ())), preferred_element_type=F32)


def _sigmoid(x):
    return 1.0 / (1.0 + jnp.exp(-x))


def _silu(x):
    return x * _sigmoid(x)


def _gelu_tanh(x):
    c = math.sqrt(2.0 / math.pi)
    return 0.5 * x * (1.0 + jnp.tanh(c * (x + 0.044715 * (x * x * x))))


def _seg_mean_sq(x, avg):
    hi, lo = _split_bf16(x * x)
    return _dot(hi, avg) + _dot(lo, avg)


def _avg_matrix(width, seg):
    idx = np.arange(width) // seg
    return jnp.asarray((idx[:, None] == idx[None, :]).astype(np.float32) / seg, dtype=BF16)


def _inproj_kernel(x_ref, g_ref, w_ref, qg_ref, kg_ref, gb_ref, avgq_ref, avgk_ref,
                   ogla_ref, os5_ref, u16_ref, *nsa_refs, seq_len):
    x = x_ref[...]
    ms = jnp.mean(x * x, axis=-1, keepdims=True)
    h = (x * lax.rsqrt(ms + RMS_EPS) * g_ref[...]).astype(BF16)
    ogla_ref[...] = _dot(h, w_ref[:, 0:GLA_COLS])
    s5 = _dot(h, w_ref[:, GLA_COLS:GLA_COLS + S5_COLS])
    os5_ref[...] = s5
    u16_ref[...] = s5[:, 0:S5_W].astype(BF16)
    _nsa_prep_body(_dot(h, w_ref[:, GLA_COLS + S5_COLS:]), qg_ref, kg_ref, gb_ref, avgq_ref, avgk_ref,
                   *nsa_refs, seq_len=seq_len)


def _inproj(x2, g_row, w_all, pp, tm, seq_len):
    m = x2.shape[0]
    ncols = w_all.shape[1]
    gw = NSA_KV * NSA_DH
    assert seq_len % tm == 0 and seq_len // SEL_BLOCK <= LANE
    qg_row, kg_rows, gbp = pp
    consts = (g_row, w_all, qg_row, kg_rows, gbp, _avg_matrix(NSA_W, NSA_DH), _avg_matrix(gw, NSA_DH))
    full = lambda a: pl.BlockSpec(a.shape, lambda i: (0,) * a.ndim)
    tok = lambda w: pl.BlockSpec((tm, w), lambda i: (i, 0))
    head = lambda n, w: pl.BlockSpec((n, tm, w), lambda i: (0, i, 0))
    outs = [(tok(GLA_COLS), (m, GLA_COLS), F32), (tok(S5_COLS), (m, S5_COLS), F32), (tok(S5_W), (m, S5_W), BF16),
            (head(NSA_HEADS, NSA_DH), (NSA_HEADS, m, NSA_DH), BF16),
            (tok(gw), (m, gw), F32), (tok(gw), (m, gw), F32),
            (head(NSA_KV, 2 * LANE), (NSA_KV, m, 2 * LANE), BF16), (head(NSA_KV, LANE), (NSA_KV, m, LANE), BF16),
            (head(NSA_KV, NSA_DH), (NSA_KV, m, NSA_DH), BF16), (head(NSA_KV, LANE), (NSA_KV, m, LANE), BF16),
            (head(NSA_KV, LANE), (NSA_KV, m, LANE), F32), (head(NSA_HEADS, NSA_DH), (NSA_HEADS, m, NSA_DH), F32)]
    return pl.pallas_call(
        functools.partial(_inproj_kernel, seq_len=seq_len),
        grid=(m // tm,),
        in_specs=[pl.BlockSpec((tm, D_MODEL), lambda i: (i, 0))] + [full(a) for a in consts],
        out_specs=[o[0] for o in outs],
        out_shape=[jax.ShapeDtypeStruct(o[1], o[2]) for o in outs],
        compiler_params=_cparams(("parallel",)),
        name="inproj",
    )(x2, *consts)


def _inproj_weight(w_in_l):
    sizes = (GLA_QK, GLA_QK, GLA_W, GLA_LOWRANK, GLA_W, S5_W, S5_W,
             NSA_W, NSA_KV_W, NSA_HEADS * N_BRANCH, NSA_W)
    offs = np.concatenate([[0], np.cumsum(sizes)])
    seg = [w_in_l[:, offs[i]:offs[i + 1]] for i in range(len(sizes))]
    gq, gk, gv, glr, gg, su, sg, nq, nkv, ngl, ng = seg

    def pad(a):
        return jnp.pad(a, ((0, 0), (0, LANE - a.shape[1])))

    return jnp.concatenate([gq, gk, gv, gg, pad(glr), su, sg, nq, nkv, ng, pad(ngl)], axis=1).astype(BF16)


GLA_LEVELS = (16, 8, 4, 2, 1)


def _gla_exponent_matrices(tt):
    r = np.arange(tt)[:, None]
    c = np.arange(tt)[None, :]
    same = (r // GLA_CHUNK) == (c // GLA_CHUNK)
    mats = [same & (c <= r), same & (c > r)]
    for s in GLA_LEVELS:
        blk_r = r // s
        odd = (blk_r % 2) == 1
        mats.append(odd & (c >= blk_r * s) & (c <= r))
        mats.append((~odd) & (c > r) & (c <= blk_r * s + s - 1))
    return jnp.asarray(np.stack(mats).astype(np.float32), dtype=BF16)


def _gla_kernel(x_ref, dm_ref, w2_ref, b2_ref, on_ref, avg_ref, y_ref, st_ref, *, tt):
    nchunk = tt // GLA_CHUNK

    @pl.when(pl.program_id(1) == 0)
    def _():
        st_ref[...] = jnp.zeros_like(st_ref)

    x = x_ref[...]
    q = x[:, 0:GLA_QK] * (GLA_DK ** -0.5)
    k = x[:, GLA_QK:2 * GLA_QK]
    v = x[:, 2 * GLA_QK:2 * GLA_QK + GLA_W]
    gate = x[:, 2 * GLA_QK + GLA_W:2 * GLA_QK + 2 * GLA_W]
    lr = x[:, 2 * GLA_QK + 2 * GLA_W:]
    z = _dot(lr.astype(BF16), w2_ref[...]) + b2_ref[...]
    glog = -(jnp.maximum(-z, 0.0) + jnp.log1p(jnp.exp(-jnp.abs(z)))) / GLA_GATE_NORM

    ghi, glo = _split_bf16(glog)
    gcat = jnp.concatenate([ghi, glo], axis=1)

    def expo(i):
        e = _dot(dm_ref[i], gcat)
        return e[:, :GLA_QK] + e[:, GLA_QK:]

    row = lax.broadcasted_iota(jnp.int32, (tt, GLA_QK), 0)
    lane = lax.broadcasted_iota(jnp.int32, (tt, GLA_QK), 1)
    rr = lax.broadcasted_iota(jnp.int32, (GLA_HEADS * tt, tt), 0) % tt
    cc = lax.broadcasted_iota(jnp.int32, (GLA_HEADS * tt, tt), 1)
    head_masks = [(lane // GLA_DK) == h for h in range(GLA_HEADS)]

    def stack_heads(a):
        return jnp.concatenate([jnp.where(m, a, 0.0) for m in head_masks], axis=0).astype(BF16)

    att = jnp.where(rr == cc, _dot_nt(stack_heads(q), k.astype(BF16)), 0.0)
    for li, s in enumerate(GLA_LEVELS):
        odd = ((row // s) % 2) == 1
        qs = jnp.where(odd, q * jnp.exp(expo(2 + 2 * li)), 0.0)
        ks = jnp.where(odd, 0.0, k * jnp.exp(expo(3 + 2 * li)))
        blk = (rr // (2 * s)) == (cc // (2 * s))
        att = att + jnp.where(blk, _dot_nt(stack_heads(qs), ks.astype(BF16)), 0.0)
    att = att.astype(BF16)

    lane_v = lax.broadcasted_iota(jnp.int32, (tt, GLA_W), 1)
    o = jnp.zeros((tt, GLA_W), F32)
    for h in range(GLA_HEADS):
        vh = jnp.where((lane_v // GLA_DV) == h, v, 0.0).astype(BF16)
        o = o + _dot(att[h * tt:(h + 1) * tt], vh)

    bcum = expo(0)
    brev = expo(1)
    qe = (q * jnp.exp(bcum)).astype(BF16)
    ke = (k * jnp.exp(brev)).astype(BF16)
    vb = v.astype(BF16)
    sr = lax.broadcasted_iota(jnp.int32, (GLA_W, GLA_QK), 0)
    sc = lax.broadcasted_iota(jnp.int32, (GLA_W, GLA_QK), 1)
    st_mask = (sr // GLA_DV) == (sc // GLA_DK)
    st = st_ref[...]
    o_inter = []
    for c in range(nchunk):
        lo, hi = c * GLA_CHUNK, (c + 1) * GLA_CHUNK
        o_inter.append(_dot_nt(qe[lo:hi], st.astype(BF16)))
        kv_t = jnp.where(st_mask, _dot_tn(vb[lo:hi], ke[lo:hi]), 0.0)
        st = st * jnp.exp(bcum[hi - 1:hi, :]) + kv_t
    st_ref[...] = st
    o = o + jnp.concatenate(o_inter, axis=0)

    ms = _seg_mean_sq(o, avg_ref[...])
    y_ref[...] = o * lax.rsqrt(ms + RMS_EPS) * on_ref[...] * _silu(gate)


def _gla_params(w2, b2, onorm):
    return (jnp.pad(w2, ((0, LANE - GLA_LOWRANK), (0, 0))).astype(BF16), b2.reshape(1, GLA_QK),
            jnp.tile(onorm, GLA_HEADS).reshape(1, GLA_W))


def _gla(ogla, bsz, gp, tt):
    m = ogla.shape[0]
    nt = m // bsz // tt
    w2p, b2_row, on_row = gp
    dmats = _gla_exponent_matrices(tt)
    return pl.pallas_call(
        functools.partial(_gla_kernel, tt=tt),
        grid=(bsz, nt),
        in_specs=[pl.BlockSpec((tt, GLA_COLS), lambda b, t: (b * nt + t, 0)),
                  pl.BlockSpec(dmats.shape, lambda b, t: (0, 0, 0)),
                  pl.BlockSpec((LANE, GLA_QK), lambda b, t: (0, 0)),
                  pl.BlockSpec((1, GLA_QK), lambda b, t: (0, 0)),
                  pl.BlockSpec((1, GLA_W), lambda b, t: (0, 0)),
                  pl.BlockSpec((GLA_W, GLA_W), lambda b, t: (0, 0))],
        out_specs=pl.BlockSpec((tt, GLA_W), lambda b, t: (b * nt + t, 0)),
        out_shape=jax.ShapeDtypeStruct((m, GLA_W), F32),
        scratch_shapes=[pltpu.VMEM((GLA_W, GLA_QK), F32)],
        compiler_params=_cparams(("arbitrary", "arbitrary")),
        name="gla",
    )(ogla, dmats, w2p, b2_row, on_row, _avg_matrix(GLA_W, GLA_DV))


def _s5_operators(lam_re, lam_im, log_step, b_re, b_im, c_re, c_im, n_super):
    lr = jnp.minimum(lam_re.astype(F32), -1e-4)
    li = lam_im.astype(F32)
    dt = jnp.exp(log_step.astype(F32))[:, None]

    def apow(tau):
        tau = jnp.asarray(tau, F32)[..., None, None]
        mag = jnp.exp(lr * dt * tau)
        return mag * jnp.cos(li * dt * tau), mag * jnp.sin(li * dt * tau)

    ar, ai = apow(1.0)
    den = lr * lr + li * li
    fr = ((ar - 1.0) * lr + ai * li) / den
    fi = (ai * lr - (ar - 1.0) * li) / den
    br, bi = b_re.astype(F32), b_im.astype(F32)
    bbr = fr[..., None] * br - fi[..., None] * bi
    bbi = fr[..., None] * bi + fi[..., None] * br
    cr, ci = c_re.astype(F32), c_im.astype(F32)
    L = S5_L
    pr, pi = apow(np.arange(L + 1))
    hp = lax.Precision.HIGHEST
    cpr = cr * pr[:L, :, None, :] - ci * pi[:L, :, None, :]
    cpi = cr * pi[:L, :, None, :] + ci * pr[:L, :, None, :]
    taps = (jnp.einsum('tgcp,gpd->tgcd', cpr, bbr, precision=hp)
            - jnp.einsum('tgcp,gpd->tgcd', cpi, bbi, precision=hp))
    eye = jnp.eye(S5_GROUPS, dtype=F32)
    nst = S5_GROUPS * S5_STATE
    kbd = jnp.einsum('tgcd,gh->tgdhc', taps, eye).reshape(L, S5_W, S5_W)
    b_in = [jnp.einsum('gpd,gh->gdhp', a, eye).reshape(S5_W, nst) for a in (bbr, bbi)]
    c_out = [jnp.einsum('gcp,gh->gphc', a, eye).reshape(nst, S5_W) for a in (cr, -ci)]
    flat = lambda a: a.reshape(a.shape[0], 1, nst)
    s_idx = np.arange(L)
    nsteps = max(1, int(math.ceil(math.log2(n_super))))
    spr, spi = apow(float(L) * (2.0 ** np.arange(nsteps)))
    return (kbd.astype(BF16), b_in[0].astype(BF16), b_in[1].astype(BF16),
            c_out[0].astype(BF16), c_out[1].astype(BF16),
            flat(pr[L - 1 - s_idx]), flat(pi[L - 1 - s_idx]),
            flat(pr[1:]), flat(pi[1:]),
            flat(spr), flat(spi))


def _s5_kernel(u_ref, kbd_ref, bre_ref, bim_ref, cre_ref, cim_ref, pinr_ref, pini_ref, poutr_ref, pouti_ref,
               spr_ref, spi_ref, y_ref, xr_ref, xi_ref, *, nsteps):
    step = lambda r: slice(r * S5_W, (r + 1) * S5_W)
    n_super = xr_ref.shape[0]
    cw = 2 * LANE
    row = lax.broadcasted_iota(jnp.int32, (n_super, cw), 0)
    for c in range(xr_ref.shape[1] // cw):
        ln = slice(c * cw, (c + 1) * cw)
        xr = xi = None
        for r in range(S5_L):
            zr = _dot(u_ref[0, :, step(r)], bre_ref[:, ln])
            zi = _dot(u_ref[0, :, step(r)], bim_ref[:, ln])
            pr, pi = pinr_ref[r][:, ln], pini_ref[r][:, ln]
            cr, ci = pr * zr - pi * zi, pr * zi + pi * zr
            xr, xi = (cr, ci) if r == 0 else (xr + cr, xi + ci)
        for kstep in range(nsteps):
            d = 1 << kstep
            sr = jnp.where(row >= d, pltpu.roll(xr, d, 0), 0.0)
            si = jnp.where(row >= d, pltpu.roll(xi, d, 0), 0.0)
            pr, pi = spr_ref[kstep][:, ln], spi_ref[kstep][:, ln]
            xr, xi = xr + pr * sr - pi * si, xi + pr * si + pi * sr
        xr_ref[:, ln] = jnp.where(row >= 1, pltpu.roll(xr, 1, 0), 0.0)
        xi_ref[:, ln] = jnp.where(row >= 1, pltpu.roll(xi, 1, 0), 0.0)
    for s in range(S5_L):
        pr, pi = poutr_ref[s], pouti_ref[s]
        er, ei = xr_ref[...], xi_ref[...]
        acc = (_dot((pr * er - pi * ei).astype(BF16), cre_ref[...])
               + _dot((pr * ei + pi * er).astype(BF16), cim_ref[...]))
        for r in range(s + 1):
            acc = acc + _dot(u_ref[0, :, step(r)], kbd_ref[s - r])
        y_ref[0, :, step(s)] = acc


def _s5_conv(u2, ops):
    bsz, n_super, width = u2.shape
    nsteps = ops[-1].shape[0]
    nst = S5_GROUPS * S5_STATE
    full = lambda a: pl.BlockSpec(a.shape, lambda b: (0,) * a.ndim)
    return pl.pallas_call(
        functools.partial(_s5_kernel, nsteps=nsteps),
        grid=(bsz,),
        in_specs=[pl.BlockSpec((1, n_super, width), lambda b: (b, 0, 0))] + [full(a) for a in ops],
        out_specs=pl.BlockSpec((1, n_super, width), lambda b: (b, 0, 0)),
        out_shape=jax.ShapeDtypeStruct((bsz, n_super, width), F32),
        scratch_shapes=[pltpu.VMEM((n_super, nst), F32), pltpu.VMEM((n_super, nst), F32)],
        compiler_params=_cparams(("arbitrary",)),
        name="s5_conv",
    )(u2, *ops)


def _nsa_prep_body(x, qg_ref, kg_ref, gb_ref, avgq_ref, avgk_ref,
                   q_ref, kc_ref, vc_ref, ks_ref, vs_ref, kw_ref, vw_ref, gt_ref, sg_ref, *, seq_len):
    q = x[:, 0:NSA_W]
    qn = q * lax.rsqrt(_seg_mean_sq(q, avgq_ref[...]) + RMS_EPS) * qg_ref[...] * (NSA_DH ** -0.5)
    for h in range(NSA_HEADS):
        q_ref[h] = qn[:, h * NSA_DH:(h + 1) * NSA_DH].astype(BF16)
    kv = x[:, NSA_W:NSA_W + NSA_KV_W]
    gw = NSA_KV * NSA_DH
    kc_ref[...] = kv[:, 0:gw]
    vc_ref[...] = kv[:, gw:2 * gw]

    def knorm(a, br):
        return a * lax.rsqrt(_seg_mean_sq(a, avgk_ref[...]) + RMS_EPS) * kg_ref[br]

    ksel = knorm(kv[:, 2 * gw:3 * gw], 1)
    vsel = kv[:, 3 * gw:4 * gw]
    kwin = knorm(kv[:, 4 * gw:5 * gw], 2)
    vwin = kv[:, 5 * gw:6 * gw]
    low = lax.broadcasted_iota(jnp.int32, vsel.shape, 1) < NSA_DH

    def with_ones(a, g):
        return jnp.where(low, a if g == 0 else pltpu.roll(a, NSA_DH, 1), 1.0).astype(BF16)

    tm = x.shape[0]
    pos = lax.rem(pl.program_id(0) * tm, seq_len) + lax.broadcasted_iota(jnp.int32, (tm, LANE), 0)
    onehot = jnp.where(lax.broadcasted_iota(jnp.int32, (tm, LANE), 1) == pos // SEL_BLOCK, 1.0, 0.0)
    for g in range(NSA_KV):
        sl = slice(g * NSA_DH, (g + 1) * NSA_DH)
        kpart = jnp.where(low, ksel if g == 0 else pltpu.roll(ksel, NSA_DH, 1), 0.0)
        ks_ref[g] = jnp.concatenate([onehot, kpart], axis=1).astype(BF16)
        vs_ref[g] = with_ones(vsel, g)
        kw_ref[g] = kwin[:, sl].astype(BF16)
        vw_ref[g] = with_ones(vwin, g)
    gate = x[:, NSA_W + NSA_KV_W:2 * NSA_W + NSA_KV_W]
    sgate = _silu(gate)
    for h in range(NSA_HEADS):
        sg_ref[h] = sgate[:, h * NSA_DH:(h + 1) * NSA_DH]
    gl = x[:, 2 * NSA_W + NSA_KV_W:]
    sig = _sigmoid(gl + gb_ref[...])
    per_group = NSA_HPG * N_BRANCH
    gt_ref[0] = sig
    gt_ref[1] = pltpu.roll(sig, LANE - per_group, 1)


def _nsa_prep_params(qn_g, kn_g, gate_b):
    gw = NSA_KV * NSA_DH
    return (jnp.tile(qn_g, NSA_HEADS).reshape(1, NSA_W), jnp.tile(kn_g, (1, NSA_KV)).reshape(N_BRANCH, 1, gw),
            jnp.pad(gate_b, (0, LANE - gate_b.shape[0])).reshape(1, LANE))


def _compress_kernel(a_ref, pt_ref, pb_ref, w1t_ref, w1b_ref, b1_ref, w2_ref, b2_ref, ng_ref, avg_ref, o_ref,
                     *, normalise):
    a = a_ref[0]
    n = a.shape[0]
    h1 = _dot((a + pt_ref[...]).astype(BF16), w1t_ref[...])
    h2 = _dot((a + pb_ref[...]).astype(BF16), w1b_ref[...])
    hid = _gelu_tanh(h1 + pltpu.roll(h2, n - 1, 0) + b1_ref[...])
    out = _dot(hid.astype(BF16), w2_ref[...]) + b2_ref[...]
    if normalise:
        out = out * lax.rsqrt(_seg_mean_sq(out, avg_ref[...]) + RMS_EPS) * ng_ref[...]
    for g in range(NSA_KV):
        o_ref[0, g] = out[:, g * NSA_DH:(g + 1) * NSA_DH].astype(BF16)


def _compress_params(pos, w1, b1, w2, b2, norm_g):
    gw = NSA_KV * NSA_DH
    half = CMP_STRIDE * NSA_DH

    def pos_rows(p):
        return jnp.tile(p[:, None, :], (1, NSA_KV, 1)).reshape(1, CMP_STRIDE * gw)

    def w1_rows(w):
        w = w.reshape(CMP_STRIDE, NSA_DH, CMP_HIDDEN)
        z = jnp.zeros_like(w)
        g0 = jnp.concatenate([w, z], axis=2)
        g1 = jnp.concatenate([z, w], axis=2)
        return jnp.stack([g0, g1], axis=1).reshape(CMP_STRIDE * gw, NSA_KV * CMP_HIDDEN).astype(BF16)

    zero = jnp.zeros_like(w2)
    w2b = jnp.concatenate([jnp.concatenate([w2, zero], axis=1), jnp.concatenate([zero, w2], axis=1)], axis=0)
    ng = jnp.tile(norm_g, NSA_KV).reshape(1, gw)
    return (pos_rows(pos[:CMP_STRIDE]), pos_rows(pos[CMP_STRIDE:]), w1_rows(w1[:half]), w1_rows(w1[half:]),
            jnp.tile(b1, NSA_KV).reshape(1, -1), w2b.astype(BF16), jnp.tile(b2, NSA_KV).reshape(1, gw), ng)


def _compress(raw, bsz, cp, normalise):
    gw = NSA_KV * NSA_DH
    n = raw.shape[0] // bsz // CMP_STRIDE
    a2 = raw.reshape(bsz, n, CMP_STRIDE * gw)
    full = lambda a: pl.BlockSpec(a.shape, lambda b: (0,) * a.ndim)
    args = tuple(cp) + (_avg_matrix(gw, NSA_DH),)
    return pl.pallas_call(
        functools.partial(_compress_kernel, normalise=normalise),
        grid=(bsz,),
        in_specs=[pl.BlockSpec((1, n, CMP_STRIDE * gw), lambda b: (b, 0, 0))] + [full(a) for a in args],
        out_specs=pl.BlockSpec((1, NSA_KV, n, NSA_DH), lambda b: (b, 0, 0, 0)),
        out_shape=jax.ShapeDtypeStruct((bsz, NSA_KV, n, NSA_DH), BF16),
        compiler_params=_cparams(("parallel",)),
        name="nsa_compress",
    )(a2, *args)


def _lane_slabs(a):
    return [a[:, c * LANE:(c + 1) * LANE] for c in range(a.shape[1] // LANE)]


def _row_max(slabs):
    mx = functools.reduce(jnp.maximum, slabs)
    return jnp.broadcast_to(jnp.max(mx, axis=1, keepdims=True), mx.shape)


def _nsa_kernel(q_ref, kc_ref, vc_ref, ovt_ref, ks_ref, vs_ref, kw_ref, vw_ref, gt_ref, sg_ref, o_ref,
                qa_ref, s0_ref, s1_ref, p0_ref, p1_ref, a0_ref, a1_ref, m_ref, acc_ref, y_ref,
                *, tq, tk, wlen, top_k):
    i = pl.program_id(2)
    rows = NSA_HPG * tq
    q4 = q_ref[...].reshape(rows, NSA_DH)
    gates = gt_ref[0]
    per_head = tq // LANE
    blocks = [(k // per_head, slice(k * LANE, (k + 1) * LANE),
               slice((k % per_head) * LANE, (k % per_head + 1) * LANE)) for k in range(rows // LANE)]

    kc = kc_ref[0, 0]
    n = kc.shape[0]
    s_cmp = _dot_nt(q4, kc)
    start = pl.multiple_of(jnp.maximum(i * tq + tq - wlen, 0), tq)
    s_win = _dot_nt(q4, kw_ref[0, pl.ds(start, wlen), :])

    c_end = lax.broadcasted_iota(jnp.int32, (LANE, n), 1) * CMP_STRIDE + (CMP_LEN - 1)
    psum = [jnp.zeros((LANE, n), F32) for _ in range(per_head)]
    for k, (_, r, tr) in enumerate(blocks):
        mask = c_end <= i * tq + tr.start + lax.broadcasted_iota(jnp.int32, (LANE, n), 0)
        sm = jnp.where(mask, s_cmp[r], NEG_INF)
        mx = jnp.max(sm, axis=1, keepdims=True)
        e = jnp.exp(sm - mx)
        scale = jnp.where(mx > 0.5 * NEG_INF, 1.0 / jnp.sum(e, axis=1, keepdims=True), 0.0)
        p = e * scale
        psum[k % per_head] = psum[k % per_head] + p
        p0_ref[r, 0:n] = p.astype(BF16)
    o_cmp = _dot(p0_ref[:, 0:n], vc_ref[0, 0])
    psum = jnp.concatenate(psum, axis=0) if per_head > 1 else psum[0]
    imp = _dot_nt(ovt_ref[...], psum.astype(BF16))

    tt = i * tq + lax.broadcasted_iota(jnp.int32, (tq, wlen), 0)
    kpos = start + lax.broadcasted_iota(jnp.int32, (tq, wlen), 1)
    wbias = jnp.where((kpos <= tt) & (kpos > tt - WINDOW), 0.0, NEG_INF)
    for _, r, tr in blocks:
        slabs = [a + b for a, b in zip(_lane_slabs(s_win[r]), _lane_slabs(wbias[tr]))]
        m_w = _row_max(slabs)
        for c, sl in enumerate(slabs):
            p1_ref[r, c * LANE:(c + 1) * LANE] = jnp.exp(sl - m_w).astype(BF16)
    acc_w = _dot(p1_ref[:, 0:wlen], vw_ref[0, pl.ds(start, wlen), :])
    o_win = acc_w[:, :NSA_DH] / pltpu.roll(acc_w, NSA_DH, 1)[:, :NSA_DH]

    sid = lax.broadcasted_iota(jnp.int32, imp.shape, 0)
    cur = (i * tq + lax.broadcasted_iota(jnp.int32, imp.shape, 1)) // SEL_BLOCK
    visible = sid <= cur
    forced = (sid == 0) | (sid == cur) | (sid == cur - 1)
    candidate = visible & jnp.logical_not(forced)
    val = jnp.where(candidate, imp, -jnp.inf)
    sidf = sid.astype(F32)
    for _ in range(max(top_k - 3, 0)):
        mval = jnp.max(val, axis=0, keepdims=True)
        first = jnp.min(jnp.where(val == mval, sidf, float(LANE)), axis=0, keepdims=True)
        val = jnp.where(sidf == first, -jnp.inf, val)
    chosen = forced | (visible & (cur < top_k)) | (candidate & (val == -jnp.inf))
    selb = jnp.where(jnp.transpose(jnp.where(chosen, 1.0, 0.0)) > 0.5, 0.0, NEG_INF).astype(BF16)

    for h, r, tr in blocks:
        c0 = h * N_BRANCH
        y_ref[r] = gates[tr, c0:c0 + 1] * o_cmp[r] + gates[tr, c0 + 2:c0 + 3] * o_win[r]
        qa_ref[r, 0:LANE] = selb[tr]
        qa_ref[r, LANE:LANE + NSA_DH] = q_ref[h, tr, :]
        qa_ref[r, LANE + NSA_DH:2 * LANE] = jnp.zeros((LANE, LANE - NSA_DH), BF16)
    m_ref[...] = jnp.full(m_ref.shape, NEG_INF, F32)
    acc_ref[...] = jnp.zeros(acc_ref.shape, F32)

    def qk(j, s_ref):
        s_ref[...] = _dot_nt(qa_ref[...], ks_ref[0, pl.ds(pl.multiple_of(j * tk, tk), tk), :])

    def soft(s_ref, p_ref, a_ref, bias=None):
        for _, r, tr in blocks:
            slabs = [s_ref[r, c * LANE:(c + 1) * LANE] for c in range(tk // LANE)]
            if bias is not None:
                slabs = [a + b for a, b in zip(slabs, _lane_slabs(bias[tr]))]
            m_prev = m_ref[r]
            m_new = jnp.maximum(m_prev, _row_max(slabs))
            for c, sl in enumerate(slabs):
                p_ref[r, c * LANE:(c + 1) * LANE] = jnp.exp(sl - m_new).astype(BF16)
            a_ref[r] = jnp.exp(m_prev - m_new)
            m_ref[r] = m_new

    def pv(j, p_ref, a_ref):
        v = vs_ref[0, pl.ds(pl.multiple_of(j * tk, tk), tk), :]
        acc_ref[...] = a_ref[...] * acc_ref[...] + _dot(p_ref[:, 0:tk], v)

    n_full = (i * tq) // tk
    qk(0, s0_ref)

    p1_ref[:, 0:tk] = jnp.zeros((rows, tk), BF16)
    a1_ref[...] = jnp.ones(a1_ref.shape, F32)

    def pair(jj, carry):
        j = 2 * jj
        pv(jnp.maximum(j - 1, 0), p1_ref, a1_ref)
        qk(j + 1, s1_ref)
        soft(s0_ref, p0_ref, a0_ref)
        pv(j, p0_ref, a0_ref)
        qk(j + 2, s0_ref)
        soft(s1_ref, p1_ref, a1_ref)
        return carry

    n_pair = n_full // 2
    lax.fori_loop(0, n_pair, pair, 0)
    pending = jnp.maximum(2 * n_pair - 1, 0)

    tt = i * tq + lax.broadcasted_iota(jnp.int32, (tq, tk), 0)
    kpos = n_full * tk + lax.broadcasted_iota(jnp.int32, (tq, tk), 1)
    causal = jnp.where(kpos <= tt, 0.0, NEG_INF)
    odd = (n_full % 2) == 1

    @pl.when(odd)
    def _():
        pv(pending, p1_ref, a1_ref)
        qk(n_full, s1_ref)
        soft(s0_ref, p0_ref, a0_ref)
        pv(n_full - 1, p0_ref, a0_ref)
        soft(s1_ref, p1_ref, a1_ref, causal)
        pv(n_full, p1_ref, a1_ref)

    @pl.when(jnp.logical_not(odd))
    def _():
        pv(pending, p1_ref, a1_ref)
        soft(s0_ref, p0_ref, a0_ref, causal)
        pv(n_full, p0_ref, a0_ref)

    acc_sel = acc_ref[...]
    o_sel = acc_sel[:, :NSA_DH] / pltpu.roll(acc_sel, NSA_DH, 1)[:, :NSA_DH]

    for h, r, tr in blocks:
        c1 = h * N_BRANCH + 1
        o_ref[h, tr, :] = ((y_ref[r] + gates[tr, c1:c1 + 1] * o_sel[r]) * sg_ref[h, tr, :]).astype(BF16)


def _overlap_t(n_cmp_pad, n_sel):
    c = np.arange(n_cmp_pad)
    start, end = c * CMP_STRIDE, c * CMP_STRIDE + CMP_LEN - 1
    s0 = np.arange(LANE) * SEL_BLOCK
    ov = (start[None, :] < s0[:, None] + SEL_BLOCK) & (end[None, :] >= s0[:, None])
    ov = ov & (np.arange(LANE) < n_sel)[:, None]
    return jnp.asarray(ov.astype(np.float32), dtype=BF16)


def _nsa_attn(qh, kc, vc, ks, vs, kw, vw, gates, sg, bsz, tq, tk):
    m = qh.shape[1]
    t = m // bsz
    nq = t // tq
    n = kc.shape[2]
    n_sel = t // SEL_BLOCK
    top_k = min(SEL_TOPK, n_sel)
    wlen = WINDOW + tq
    assert t % tk == 0 and tk % tq == 0 and WINDOW % tq == 0 and t >= wlen and tk % LANE == 0
    assert n_sel <= LANE and n % LANE == 0 and tq % LANE == 0
    ovt = _overlap_t(n, n_sel)
    rows = NSA_HPG * tq
    pw = max(n, tk, wlen)
    heads = pl.BlockSpec((NSA_HPG, tq, NSA_DH), lambda b, g, i: (g, b * nq + i, 0))
    cmp_spec = pl.BlockSpec((1, 1, n, NSA_DH), lambda b, g, i: (b, g, 0, 0))
    seq = lambda w: pl.BlockSpec((1, t, w), lambda b, g, i: (g, b, 0))
    stat = lambda: pltpu.VMEM((rows, LANE), F32)
    return pl.pallas_call(
        functools.partial(_nsa_kernel, tq=tq, tk=tk, wlen=wlen, top_k=top_k),
        grid=(bsz, NSA_KV, nq),
        in_specs=[heads, cmp_spec, cmp_spec,
                  pl.BlockSpec((LANE, n), lambda b, g, i: (0, 0)),
                  seq(2 * LANE), seq(LANE), seq(NSA_DH), seq(LANE),
                  pl.BlockSpec((1, tq, LANE), lambda b, g, i: (g, b * nq + i, 0)),
                  heads],
        out_specs=heads,
        out_shape=jax.ShapeDtypeStruct((NSA_HEADS, m, NSA_DH), BF16),
        scratch_shapes=[pltpu.VMEM((rows, 2 * LANE), BF16),
                        pltpu.VMEM((rows, tk), F32), pltpu.VMEM((rows, tk), F32),
                        pltpu.VMEM((rows, pw), BF16), pltpu.VMEM((rows, pw), BF16),
                        stat(), stat(), stat(), stat(), pltpu.VMEM((rows, NSA_DH), F32)],
        compiler_params=_cparams(("parallel", "parallel", "arbitrary")),
        name="nsa_attn",
    )(qh, kc, vc, ovt, ks, vs, kw, vw, gates, sg)


def _outproj_kernel(x_ref, ygla_ref, yconv_ref, os5_ref, d_ref, gw_ref, gb_ref, ynsa_ref, wo_ref, o_ref):
    u = os5_ref[:, 0:S5_W]
    sgate = os5_ref[:, S5_W:2 * S5_W]
    y = _gelu_tanh(yconv_ref[...] + d_ref[...] * u)
    z = _dot(y.astype(BF16), gw_ref[...]) + gb_ref[...]
    ys5 = z[:, :S5_W] * _sigmoid(z[:, S5_W:]) * _silu(sgate)
    acc = x_ref[...] + _dot(ygla_ref[...].astype(BF16), wo_ref[0:GLA_W, :])
    acc = acc + _dot(ys5.astype(BF16), wo_ref[GLA_W:GLA_W + S5_W, :])
    base = GLA_W + S5_W
    for h in range(NSA_HEADS):
        acc = acc + _dot(ynsa_ref[h], wo_ref[base + h * NSA_DH:base + (h + 1) * NSA_DH, :])
    o_ref[...] = acc


def _outproj_params(s5_d, glu_w, glu_b, w_out):
    return s5_d.reshape(1, S5_W), glu_w.astype(BF16), glu_b.reshape(1, 2 * S5_W), w_out.astype(BF16)


def _outproj(x2, ygla, yconv, os5, ynsa, op, tm):
    m = x2.shape[0]
    tok = lambda w: pl.BlockSpec((tm, w), lambda i: (i, 0))
    head = pl.BlockSpec((NSA_HEADS, tm, NSA_DH), lambda i: (0, i, 0))
    full = lambda a: pl.BlockSpec(a.shape, lambda i: (0,) * a.ndim)
    d, gw, gb, wo = op
    return pl.pallas_call(
        _outproj_kernel,
        grid=(m // tm,),
        in_specs=[tok(D_MODEL), tok(GLA_W), tok(S5_W), tok(S5_COLS), full(d), full(gw), full(gb),
                  head, full(wo)],
        out_specs=tok(D_MODEL),
        out_shape=jax.ShapeDtypeStruct((m, D_MODEL), F32),
        compiler_params=_cparams(("parallel",)),
        name="outproj",
    )(x2, ygla, yconv, os5, d, gw, gb, ynsa, wo)


def _pick(t, pref):
    while t % pref:
        pref //= 2
    return pref


def _layer_params(p, n_super):
    return dict(
        inproj=(p['norm_g'].reshape(1, D_MODEL), _inproj_weight(p['w_in'])),
        gla=_gla_params(p['gla_w2'], p['gla_b2'], p['gla_onorm']),
        s5=_s5_operators(p['s5_lam_re'], p['s5_lam_im'], p['s5_log_step'], p['s5_b_re'], p['s5_b_im'],
                         p['s5_c_re'], p['s5_c_im'], n_super),
        prep=_nsa_prep_params(p['nsa_qn'], p['nsa_kn'], p['nsa_gate_b']),
        cmp_k=_compress_params(p['nsa_cmp_pos'][0], p['nsa_cmp_w1'][0], p['nsa_cmp_b1'][0],
                               p['nsa_cmp_w2'][0], p['nsa_cmp_b2'][0], p['nsa_kn'][0]),
        cmp_v=_compress_params(p['nsa_cmp_pos'][1], p['nsa_cmp_w1'][1], p['nsa_cmp_b1'][1],
                               p['nsa_cmp_w2'][1], p['nsa_cmp_b2'][1], jnp.ones((NSA_DH,), F32)),
        out=_outproj_params(p['s5_d'], p['s5_glu_w'], p['s5_glu_b'], p['w_out']))


def _layer(x2, bsz, lp):
    m = x2.shape[0]
    t = m // bsz
    tm = _pick(m, 512)
    ogla, os5, u16, qh, kc_raw, vc_raw, ks, vs, kw, vw, gates, sg = _inproj(x2, *lp['inproj'], lp['prep'], tm, t)

    ygla = _gla(ogla, bsz, lp['gla'], _pick(t, 256))

    yconv = _s5_conv(u16.reshape(bsz, t // S5_L, S5_L * S5_W), lp['s5']).reshape(m, S5_W)

    kc = _compress(kc_raw, bsz, lp['cmp_k'], True)
    vc = _compress(vc_raw, bsz, lp['cmp_v'], False)
    ynsa = _nsa_attn(qh, kc, vc, ks, vs, kw, vw, gates, sg, bsz, 256, 512)

    return _outproj(x2, ygla, yconv, os5, ynsa, lp['out'], tm)


def kernel(x, norm_g, w_in, gla_w2, gla_b2, gla_onorm, s5_lam_re, s5_lam_im, s5_log_step, s5_b_re, s5_b_im,
           s5_c_re, s5_c_im, s5_d, s5_glu_w, s5_glu_b, nsa_gate_b, nsa_qn, nsa_kn, nsa_cmp_pos, nsa_cmp_w1,
           nsa_cmp_b1, nsa_cmp_w2, nsa_cmp_b2, w_out):
    params = dict(norm_g=norm_g, w_in=w_in, gla_w2=gla_w2, gla_b2=gla_b2, gla_onorm=gla_onorm,
                  s5_lam_re=s5_lam_re, s5_lam_im=s5_lam_im, s5_log_step=s5_log_step, s5_b_re=s5_b_re,
                  s5_b_im=s5_b_im, s5_c_re=s5_c_re, s5_c_im=s5_c_im, s5_d=s5_d, s5_glu_w=s5_glu_w,
                  s5_glu_b=s5_glu_b, nsa_gate_b=nsa_gate_b, nsa_qn=nsa_qn, nsa_kn=nsa_kn,
                  nsa_cmp_pos=nsa_cmp_pos, nsa_cmp_w1=nsa_cmp_w1, nsa_cmp_b1=nsa_cmp_b1,
                  nsa_cmp_w2=nsa_cmp_w2, nsa_cmp_b2=nsa_cmp_b2, w_out=w_out)
    bsz, t, d = x.shape
    x2 = x.reshape(bsz * t, d)
    prepared = jax.vmap(functools.partial(_layer_params, n_super=t // S5_L))(params)
    for layer in range(w_in.shape[0]):
        x2 = _layer(x2, bsz, jax.tree.map(lambda a: a[layer], prepared))
    return x2.reshape(bsz, t, d)
```
